```python
import jax
import jax.numpy as jnp
from jax import lax
import numpy as np

D_MODEL = 1024
BATCH = 4
SEQ = 8192
DEPTH = 4

N_META = 16
LRU_WIDTH = 512
LRU_HEADS = 8
LRU_HEAD_DIM = LRU_WIDTH // LRU_HEADS
CONV_WIDTH = 4
LRU_C = 8.0
MLA_HEADS = 8
Q_LORA = 384
KV_LORA = 256
QK_NOPE = 64
QK_ROPE = 32
QK_DIM = QK_NOPE + QK_ROPE
V_DIM = 64
MLA_WIDTH = MLA_HEADS * V_DIM
ROPE_THETA = 10000.0
ATTN_BLOCK = 128
D_MIX = LRU_WIDTH + MLA_WIDTH
SPLIT_1 = LRU_WIDTH
SPLIT_2 = 2 * LRU_WIDTH
SPLIT_3 = 2 * LRU_WIDTH + Q_LORA
SPLIT_4 = 2 * LRU_WIDTH + Q_LORA + KV_LORA
D_IN = 2 * LRU_WIDTH + Q_LORA + KV_LORA + QK_ROPE
PEER_HEADS = 8
N_KEYS = 128
N_EXPERTS = N_KEYS * N_KEYS
PEER_TOPK = 16
D_QUERY = 256
D_SUBKEY = D_QUERY // 2
PEER_CHUNK = 128
EPS = 1e-6
NEG_INF = -1e30

kernel_name = "hymba_rglru_mla_peer_trunk"


def rms_norm(x, g):
    xf = x.astype(jnp.float32)
    y = xf * lax.rsqrt(jnp.mean(xf * xf, axis=-1, keepdims=True) + EPS)
    return (y * g.astype(jnp.float32)).astype(x.dtype)


def rope(x, pos):
    half = x.shape[-1] // 2
    freq = ROPE_THETA ** (-jnp.arange(half, dtype=jnp.float32) / half)
    ang = pos[:, None] * freq[None, :]
    cos = jnp.cos(ang)[:, None, :]
    sin = jnp.sin(ang)[:, None, :]
    xf = x.astype(jnp.float32)
    x1, x2 = xf[..., :half], xf[..., half:]
    return jnp.concatenate([x1 * cos - x2 * sin, x2 * cos + x1 * sin], axis=-1).astype(x.dtype)


def causal_conv(x, w, b):
    T = x.shape[1]
    xp = jnp.pad(x, ((0, 0), (CONV_WIDTH - 1, 0), (0, 0)))
    out = xp[:, 0:T] * w[0]
    for k in range(1, CONV_WIDTH):
        out = out + xp[:, k:k + T] * w[k]
    return out + b


def rg_lru(xc, wa, ba, wx, bx, lam):
    B, T, W = xc.shape
    xb = xc.reshape(B, T, LRU_HEADS, LRU_HEAD_DIM)
    r = jax.nn.sigmoid(jnp.einsum('bthi,hij->bthj', xb, wa) + ba).reshape(B, T, W)
    i = jax.nn.sigmoid(jnp.einsum('bthi,hij->bthj', xb, wx) + bx).reshape(B, T, W)
    log_a = -LRU_C * r.astype(jnp.float32) * jax.nn.softplus(-lam.astype(jnp.float32))
    a = jnp.exp(log_a)
    mult = jnp.sqrt(-jnp.expm1(2.0 * log_a))
    bterm = mult * (i * xc).astype(jnp.float32)

    def combine(left, right):
        a1, b1 = left
        a2, b2 = right
        return a1 * a2, a2 * b1 + b2

    _, h = lax.associative_scan(combine, (a, bterm), axis=1)
    return h.astype(xc.dtype)


def causal_block_attention(q, k, v):
    B, T, H, _ = q.shape
    n_pad = (-T) % ATTN_BLOCK
    padw = ((0, 0), (n_pad, 0), (0, 0), (0, 0))
    q, k, v = jnp.pad(q, padw), jnp.pad(k, padw), jnp.pad(v, padw)
    Tp = T + n_pad
    nb = Tp // ATTN_BLOCK
    qb = jnp.moveaxis(q.reshape(B, nb, ATTN_BLOCK, H, QK_DIM), 1, 0)
    kpos = jnp.arange(Tp)
    scale = 1.0 / np.sqrt(QK_DIM).astype(np.float32)

    def one_block(args):
        q_blk, b_idx = args
        qpos = b_idx * ATTN_BLOCK + jnp.arange(ATTN_BLOCK)
        s = jnp.einsum('bqhd,bkhd->bhqk', q_blk, k).astype(jnp.float32) * scale
        mask = (kpos[None, :] <= qpos[:, None]) & (kpos[None, :] >= n_pad)
        s = jnp.where(mask[None, None], s, NEG_INF)
        p = jax.nn.softmax(s, axis=-1).astype(v.dtype)
        return jnp.einsum('bhqk,bkhd->bqhd', p, v)

    o = lax.map(one_block, (qb, jnp.arange(nb)))
    o = jnp.moveaxis(o, 0, 1).reshape(B, Tp, H, V_DIM)
    return o[:, n_pad:]


def mla_group(c_q, c_kv, k_r, q_ln_g, w_uq, kv_ln_g, w_ukv, q_hn_g, k_hn_g):
    B, T, _ = c_q.shape
    pos = jnp.arange(T, dtype=jnp.float32)
    q = (rms_norm(c_q, q_ln_g) @ w_uq).reshape(B, T, MLA_HEADS, QK_DIM)
    kv = (rms_norm(c_kv, kv_ln_g) @ w_ukv).reshape(B, T, MLA_HEADS, QK_NOPE + V_DIM)
    k_nope, v = kv[..., :QK_NOPE], kv[..., QK_NOPE:]
    k_rope = jnp.broadcast_to(k_r[:, :, None, :], (B, T, MLA_HEADS, QK_ROPE))
    k = jnp.concatenate([k_nope, k_rope], axis=-1)
    q = rms_norm(q, q_hn_g)
    k = rms_norm(k, k_hn_g)
    q = jnp.concatenate([q[..., :QK_NOPE], rope(q[..., QK_NOPE:], pos)], axis=-1)
    k = jnp.concatenate([k[..., :QK_NOPE], rope(k[..., QK_NOPE:], pos)], axis=-1)
    o = causal_block_attention(q, k, v)
    return o.reshape(B, T, MLA_WIDTH)


def peer(h, wq, subkeys, u, v):
    B, T, D = h.shape
    q = (h @ wq).reshape(B, T, PEER_HEADS, 2, D_SUBKEY)
    s = jnp.einsum('bthpd,pkd->bthpk', q, subkeys).astype(jnp.float32)
    top_s, top_i = lax.top_k(s, PEER_TOPK)
    cand_s = (top_s[..., 0, :, None] + top_s[..., 1, None, :]).reshape(B, T, PEER_HEADS, PEER_TOPK * PEER_TOPK)
    cand_i = (top_i[..., 0, :, None] * N_KEYS + top_i[..., 1, None, :]).reshape(B, T, PEER_HEADS, PEER_TOPK * PEER_TOPK)
    best_s, best_pos = lax.top_k(cand_s, PEER_TOPK)
    experts = jnp.take_along_axis(cand_i, best_pos, axis=-1)
    gates = jax.nn.softmax(best_s, axis=-1).astype(h.dtype)
    n_tok = B * T
    n_chunks = -(-n_tok // PEER_CHUNK)
    pad = n_chunks * PEER_CHUNK - n_tok
    hk = PEER_HEADS * PEER_TOPK
    hf = jnp.pad(h.reshape(n_tok, D), ((0, pad), (0, 0))).reshape(n_chunks, PEER_CHUNK, D)
    ef = jnp.pad(experts.reshape(n_tok, hk), ((0, pad), (0, 0))).reshape(n_chunks, PEER_CHUNK, hk)
    gf = jnp.pad(gates.reshape(n_tok, hk), ((0, pad), (0, 0))).reshape(n_chunks, PEER_CHUNK, hk)

    def one_chunk(args):
        hc, ec, gc = args
        uc = jnp.take(u, ec, axis=0)
        act = jax.nn.gelu(jnp.einsum('cd,ckd->ck', hc, uc), approximate=False)
        vc = jnp.take(v, ec, axis=0)
        return jnp.einsum('ck,ckd->cd', gc * act, vc)

    out = lax.map(one_chunk, (hf, ef, gf)).reshape(n_chunks * PEER_CHUNK, D)
    return out[:n_tok].reshape(B, T, D)


def setup_inputs(seed: int = 0) -> dict:
    key = jax.random.key(seed)
    ks = jax.random.split(key, 32)
    f32 = jnp.float32

    def nrm(k, shape, scale):
        return jax.random.normal(k, shape, f32) * scale

    def gain(k, n):
        return 1.0 + 0.01 * jax.random.normal(k, (DEPTH, n), f32)

    u0 = jax.random.uniform(ks[9], (DEPTH, LRU_WIDTH), f32, 0.9, 0.999)
    a0 = u0 ** (1.0 / LRU_C)
    lru_lambda = jnp.log(a0) - jnp.log1p(-a0)
    return {
        "x": nrm(ks[0], (BATCH, SEQ, D_MODEL), 1.0),
        "meta_tokens": nrm(ks[1], (N_META, D_MODEL), 1.0),
        "mix_norm_g": gain(ks[2], D_MODEL),
        "w_in": nrm(ks[3], (DEPTH, D_MODEL, D_IN), D_MODEL ** -0.5),
        "conv_w": nrm(ks[4], (DEPTH, CONV_WIDTH, LRU_WIDTH), CONV_WIDTH ** -0.5),
        "conv_b": nrm(ks[5], (DEPTH, LRU_WIDTH), 0.01),
        "lru_wa": nrm(ks[6], (DEPTH, LRU_HEADS, LRU_HEAD_DIM, LRU_HEAD_DIM), LRU_HEAD_DIM ** -0.5),
        "lru_ba": nrm(ks[7], (DEPTH, LRU_HEADS, LRU_HEAD_DIM), 0.01),
        "lru_wx": nrm(ks[8], (DEPTH, LRU_HEADS, LRU_HEAD_DIM, LRU_HEAD_DIM), LRU_HEAD_DIM ** -0.5),
        "lru_bx": nrm(ks[10], (DEPTH, LRU_HEADS, LRU_HEAD_DIM), 0.01),
        "lru_lambda": lru_lambda,
        "q_lora_norm_g": gain(ks[11], Q_LORA),
        "w_uq": nrm(ks[12], (DEPTH, Q_LORA, MLA_HEADS * QK_DIM), Q_LORA ** -0.5),
        "kv_lora_norm_g": gain(ks[13], KV_LORA),
        "w_ukv": nrm(ks[14], (DEPTH, KV_LORA, MLA_HEADS * (QK_NOPE + V_DIM)), KV_LORA ** -0.5),
        "q_head_norm_g": gain(ks[15], QK_DIM),
        "k_head_norm_g": gain(ks[16], QK_DIM),
        "lru_out_norm_g": gain(ks[17], LRU_WIDTH),
        "attn_out_norm_g": gain(ks[18], MLA_WIDTH),
        "w_out": nrm(ks[19], (DEPTH, D_MIX, D_MODEL), D_MIX ** -0.5),
        "ffn_norm_g": gain(ks[20], D_MODEL),
        "peer_wq": nrm(ks[21], (DEPTH, D_MODEL, PEER_HEADS * D_QUERY), D_MODEL ** -0.5),
        "peer_subkeys": nrm(ks[22], (DEPTH, 2, N_KEYS, D_SUBKEY), D_SUBKEY ** -0.5),
        "peer_u": nrm(ks[23], (DEPTH, N_EXPERTS, D_MODEL), D_MODEL ** -0.5),
        "peer_v": nrm(ks[24], (DEPTH, N_EXPERTS, D_MODEL), D_MODEL ** -0.5),
    }


def reference(x, meta_tokens, mix_norm_g, w_in, conv_w, conv_b, lru_wa, lru_ba, lru_wx, lru_bx,
              lru_lambda, q_lora_norm_g, w_uq, kv_lora_norm_g, w_ukv, q_head_norm_g, k_head_norm_g,
              lru_out_norm_g, attn_out_norm_g, w_out, ffn_norm_g, peer_wq, peer_subkeys, peer_u, peer_v):
    B = x.shape[0]
    meta = jnp.broadcast_to(meta_tokens.astype(x.dtype)[None], (B, N_META, D_MODEL))
    h = jnp.concatenate([meta, x], axis=1)
    for l in range(DEPTH):
        z = rms_norm(h, mix_norm_g[l]) @ w_in[l]
        x_lru = z[..., :SPLIT_1]
        gate = z[..., SPLIT_1:SPLIT_2]
        c_q = z[..., SPLIT_2:SPLIT_3]
        c_kv = z[..., SPLIT_3:SPLIT_4]
        k_r = z[..., SPLIT_4:]
        xc = causal_conv(x_lru, conv_w[l], conv_b[l])
        y_lru = rg_lru(xc, lru_wa[l], lru_ba[l], lru_wx[l], lru_bx[l], lru_lambda[l]) * jax.nn.gelu(gate, approximate=False)
        y_att = mla_group(c_q, c_kv, k_r, q_lora_norm_g[l], w_uq[l], kv_lora_norm_g[l], w_ukv[l],
                          q_head_norm_g[l], k_head_norm_g[l])
        mixed = jnp.concatenate([rms_norm(y_lru, lru_out_norm_g[l]), rms_norm(y_att, attn_out_norm_g[l])], axis=-1)
        h = h + mixed @ w_out[l]
        h = h + peer(rms_norm(h, ffn_norm_g[l]), peer_wq[l], peer_subkeys[l], peer_u[l], peer_v[l])
    return h[:, N_META:]
```

```python
import functools
import math

import jax
import jax.numpy as jnp
from jax import lax
from jax.experimental import pallas as pl
from jax.experimental.pallas import tpu as pltpu

N_META = 16
LRU_WIDTH = 512
LRU_HEADS = 8
LRU_HEAD_DIM = LRU_WIDTH // LRU_HEADS
CONV_WIDTH = 4
LRU_C = 8.0
MLA_HEADS = 8
Q_LORA = 384
KV_LORA = 256
QK_NOPE = 64
QK_ROPE = 32
QK_DIM = QK_NOPE + QK_ROPE
V_DIM = 64
ROPE_THETA = 10000.0
ATTN_BLOCK = 128
PEER_HEADS = 8
N_KEYS = 128
PEER_TOPK = 16
D_SUBKEY = 128
EPS = 1e-6
NEG_INF = -1e30

LANE = 128
SLOT = LANE
MASK_LANE = QK_DIM
ONES_LANE = V_DIM
M_INIT = -3.0e38
VMEM_LIMIT = 56 * 1024 * 1024

F32 = jnp.float32
BF16 = jnp.bfloat16
INV_SQRT2 = 1.0 / math.sqrt(2.0)


def _gelu(x):
    return 0.5 * x * (1.0 + lax.erf(x * INV_SQRT2))


def _rms(x, g, n=None):
    n = x.shape[-1] if n is None else n
    ms = jnp.sum(x * x, axis=-1, keepdims=True) * (1.0 / n)
    return x * lax.rsqrt(ms + EPS) * g


def _dot(a, b):
    return jnp.dot(a, b, preferred_element_type=F32)


def _dot_nt(a, b):
    return lax.dot_general(a, b, (((1,), (1,)), ((), ())), preferred_element_type=F32)


def _params(*sem):
    return pltpu.CompilerParams(dimension_semantics=sem, vmem_limit_bytes=VMEM_LIMIT)


def _inproj_kernel(h_ref, g_ref, win_ref, qg_ref, wuq_ref, kvg_ref, wuk_ref, wuv_ref, qhg_ref, khg_ref,
                   rc_ref, rs1_ref, rs2_ref, xl_ref, gg_ref, q_ref, k_ref, v_ref, *, tm, n_pad):
    h = h_ref[...]
    hn = _rms(h, g_ref[...])
    z = _dot(hn.astype(BF16), win_ref[...])
    o1 = LRU_WIDTH
    o2 = 2 * LRU_WIDTH
    o3 = o2 + Q_LORA
    o4 = o3 + KV_LORA
    xl_ref[...] = z[:, :o1]
    gg_ref[...] = _gelu(z[:, o1:o2])
    cqn = _rms(z[:, o2:o3], qg_ref[...]).astype(BF16)
    ckvn = _rms(z[:, o3:o4], kvg_ref[...]).astype(BF16)
    kr = z[:, o4:o4 + SLOT]
    q = _dot(cqn, wuq_ref[...])
    kn = _dot(ckvn, wuk_ref[...])
    v = _dot(ckvn, wuv_ref[...])
    rc = rc_ref[...]
    rs1 = rs1_ref[...]
    rs2 = rs2_ref[...]
    half = QK_ROPE // 2
    lane = lax.broadcasted_iota(jnp.int32, (1, SLOT), 1)
    mask_lane = (lane == MASK_LANE).astype(F32)
    ones_lane = (lane == ONES_LANE).astype(F32)
    row = pl.program_id(1) * tm + lax.broadcasted_iota(jnp.int32, (tm, 1), 0)
    kbias = jnp.where(row < n_pad, NEG_INF, 0.0) * mask_lane
    scale = 1.0 / math.sqrt(QK_DIM)

    def rope(x):
        return x * rc + pltpu.roll(x, SLOT - half, 1) * rs1 + pltpu.roll(x, half, 1) * rs2

    for hd in range(MLA_HEADS):
        sl = slice(hd * SLOT, (hd + 1) * SLOT)
        qs = rope(_rms(q[:, sl], qhg_ref[...], QK_DIM))
        q_ref[hd] = (qs * scale + mask_lane).astype(BF16)
        ks = rope(_rms(kn[:, sl] + kr, khg_ref[...], QK_DIM))
        k_ref[hd] = (ks + kbias).astype(BF16)
        v_ref[hd] = (v[:, sl] + ones_lane).astype(BF16)


def _inproj(h, p, rope_tabs, *, tm, n_pad):
    B, Tp, D = h.shape
    nb = Tp // tm
    H = MLA_HEADS
    full = lambda a: pl.BlockSpec(a.shape, lambda b, i: (0,) * a.ndim)
    tok = lambda w: pl.BlockSpec((None, tm, w), lambda b, i: (b, i, 0))
    head = pl.BlockSpec((None, H, tm, SLOT), lambda b, i: (b, 0, i, 0))
    tab = pl.BlockSpec((tm, SLOT), lambda b, i: (i, 0))
    ws = [p["mix_g"], p["w_in"], p["q_g"], p["w_uq"], p["kv_g"], p["w_uk"], p["w_uv"], p["qh_g"], p["kh_g"]]
    return pl.pallas_call(
        functools.partial(_inproj_kernel, tm=tm, n_pad=n_pad),
        grid=(B, nb),
        in_specs=[tok(D)] + [full(w) for w in ws] + [tab, tab, tab],
        out_specs=[tok(LRU_WIDTH), tok(LRU_WIDTH), head, head, head],
        out_shape=[jax.ShapeDtypeStruct((B, Tp, LRU_WIDTH), F32), jax.ShapeDtypeStruct((B, Tp, LRU_WIDTH), F32)]
        + [jax.ShapeDtypeStruct((B, H, Tp, SLOT), BF16)] * 3,
        compiler_params=_params("parallel", "parallel"),
        name="inproj",
    )(h, *ws, *rope_tabs)


def _lru_kernel(xl_ref, gg_ref, cw_ref, cb_ref, wa_ref, ba_ref, wx_ref, bx_ref, lam_ref, og_ref, y_ref,
                ext_ref, hst_ref, *, tt, n_pad):
    t = pl.program_id(1)
    HALO = 8

    @pl.when(t == 0)
    def _():
        ext_ref[0:HALO, :] = jnp.zeros((HALO, LRU_WIDTH), F32)
        hst_ref[...] = jnp.zeros((HALO, LRU_WIDTH), F32)

    row = t * tt + lax.broadcasted_iota(jnp.int32, (tt, 1), 0)
    real = row >= n_pad
    ext_ref[HALO:HALO + tt, :] = jnp.where(real, xl_ref[...], 0.0)
    xc = cb_ref[...]
    for k in range(CONV_WIDTH):
        off = HALO - (CONV_WIDTH - 1) + k
        xc = xc + ext_ref[off:off + tt, :] * cw_ref[k:k + 1, :]
    xcb = xc.astype(BF16)
    r = jax.nn.sigmoid(_dot(xcb, wa_ref[...]) + ba_ref[...])
    i = jax.nn.sigmoid(_dot(xcb, wx_ref[...]) + bx_ref[...])
    nl = -lam_ref[...]
    softplus = jnp.maximum(nl, 0.0) + jnp.log1p(jnp.exp(-jnp.abs(nl)))
    log_a = -LRU_C * r * softplus
    a = jnp.exp(log_a)
    mult = jnp.sqrt(1.0 - jnp.exp(2.0 * log_a))
    b = jnp.where(real, mult * (i * xc), 0.0)
    ridx = lax.broadcasted_iota(jnp.int32, (tt, 1), 0)
    s = 1
    while s < tt:
        keep = ridx >= s
        a_sh = jnp.where(keep, pltpu.roll(a, s, 0), 1.0)
        b_sh = jnp.where(keep, pltpu.roll(b, s, 0), 0.0)
        b = a * b_sh + b
        a = a * a_sh
        s *= 2
    hprev = hst_ref[HALO - 1:HALO, :]
    hs = a * hprev + b
    hst_ref[...] = hs[tt - HALO:tt, :]
    ext_ref[0:HALO, :] = ext_ref[tt:tt + HALO, :]
    y = hs * gg_ref[...]
    y_ref[...] = _rms(y, og_ref[...]).astype(BF16)


def _lru(xl, gg, p, *, tt, n_pad):
    B, Tp, W = xl.shape
    full = lambda a: pl.BlockSpec(a.shape, lambda b, i: (0,) * a.ndim)
    tok = pl.BlockSpec((None, tt, W), lambda b, i: (b, i, 0))
    ws = [p["conv_w"], p["conv_b"], p["wa"], p["ba"], p["wx"], p["bx"], p["lam"], p["lru_og"]]
    return pl.pallas_call(
        functools.partial(_lru_kernel, tt=tt, n_pad=n_pad),
        grid=(B, Tp // tt),
        in_specs=[tok, tok] + [full(w) for w in ws],
        out_specs=tok,
        out_shape=jax.ShapeDtypeStruct((B, Tp, W), BF16),
        scratch_shapes=[pltpu.VMEM((tt + 8, W), F32), pltpu.VMEM((8, W), F32)],
        compiler_params=_params("parallel", "arbitrary"),
        name="rglru",
    )(xl, gg, *ws)


def _attn_kernel(q_ref, k_ref, v_ref, o_ref, *, tq, tk):
    qi = pl.program_id(2)
    q = q_ref[...]
    kd = (qi * tq) // tk

    def step(kb, carry, diagonal):
        m, acc = carry
        start = pl.multiple_of(kb * tk, tk)
        s = _dot_nt(q, k_ref[pl.ds(start, tk), :])
        if diagonal:
            qpos = qi * tq + lax.broadcasted_iota(jnp.int32, (tq, tk), 0)
            kpos = kb * tk + lax.broadcasted_iota(jnp.int32, (tq, tk), 1)
            s = jnp.where(kpos <= qpos, s, NEG_INF)
        m_new = jnp.maximum(m, jnp.max(s, axis=-1, keepdims=True))
        alpha = jnp.exp(m - m_new)
        pr = jnp.exp(s - m_new).astype(BF16)
        acc = acc * alpha + _dot(pr, v_ref[pl.ds(start, tk), :])
        return m_new, acc

    init = (jnp.full((tq, 1), M_INIT, F32), jnp.zeros((tq, SLOT), F32))
    m, acc = lax.fori_loop(0, kd, lambda kb, c: step(kb, c, False), init)
    m, acc = step(kd, (m, acc), True)
    lane = lax.broadcasted_iota(jnp.int32, (1, SLOT), 1)
    o = acc / acc[:, ONES_LANE:ONES_LANE + 1]
    o_ref[...] = jnp.where(lane < V_DIM, o, 0.0).astype(BF16)


def _attention(q, k, v, *, tq, tk):
    B, H, Tp, _ = q.shape
    qspec = pl.BlockSpec((None, None, tq, SLOT), lambda b, h, i: (b, h, i, 0))
    kvspec = pl.BlockSpec((None, None, Tp, SLOT), lambda b, h, i: (b, h, 0, 0))
    return pl.pallas_call(
        functools.partial(_attn_kernel, tq=tq, tk=tk),
        grid=(B, H, Tp // tq),
        in_specs=[qspec, kvspec, kvspec],
        out_specs=qspec,
        out_shape=jax.ShapeDtypeStruct((B, H, Tp, SLOT), BF16),
        compiler_params=_params("parallel", "parallel", "arbitrary"),
        name="attention",
    )(q, k, v)


def _extract_top(x, n, payload=None):
    R, L = x.shape
    pos = lax.broadcasted_iota(jnp.int32, (R, L), 0)
    out_row = lax.broadcasted_iota(jnp.int32, (n, L), 0)
    vals = jnp.zeros((n, L), F32)
    idxs = jnp.zeros((n, L), jnp.int32)
    for it in range(n):
        m = jnp.max(x, axis=0, keepdims=True)
        idx = jnp.min(jnp.where(x == m, pos, R), axis=0, keepdims=True)
        hit = pos == idx
        if payload is None:
            sel = idx
        else:
            sel = jnp.sum(jnp.where(hit, payload, 0), axis=0, keepdims=True)
        vals = jnp.where(out_row == it, m, vals)
        idxs = jnp.where(out_row == it, sel, idxs)
        x = jnp.where(hit, -jnp.inf, x)
    return vals, idxs


def _outproj_kernel(h_ref, yl_ref, o_ref, ag_ref, wol_ref, woa_ref, fg_ref, wq_ref, sk_ref,
                    h2_ref, hn_ref, et_ref, gt_ref, qp_ref, s_ref, *, tm):
    H = MLA_HEADS
    ssq = jnp.zeros((tm, 1), F32)
    for hd in range(H):
        oh = o_ref[hd].astype(F32)
        ssq = ssq + jnp.sum(oh * oh, axis=-1, keepdims=True)
    inv = lax.rsqrt(ssq * (1.0 / (H * V_DIM)) + EPS)
    mix = _dot(yl_ref[...], wol_ref[...])
    for hd in range(H):
        an = (o_ref[hd].astype(F32) * inv * ag_ref[hd]).astype(BF16)
        mix = mix + _dot(an, woa_ref[hd])
    h2 = h_ref[...] + mix
    h2_ref[...] = h2
    hn = _rms(h2, fg_ref[...]).astype(BF16)
    hn_ref[...] = hn
    qp_ref[...] = _dot(hn, wq_ref[...])
    K = PEER_TOPK
    nchunk = tm // LANE

    def head_body(hd, carry):
        for pp in range(2):
            off = pl.multiple_of((hd * 2 + pp) * D_SUBKEY, D_SUBKEY)
            qs = qp_ref[:, pl.ds(off, D_SUBKEY)].astype(BF16)
            s_ref[pp] = _dot_nt(sk_ref[pp], qs)

        def chunk_body(j, c2):
            lo = pl.multiple_of(j * LANE, LANE)
            ts0, ti0 = _extract_top(s_ref[0, :, pl.ds(lo, LANE)], K)
            ts1, ti1 = _extract_top(s_ref[1, :, pl.ds(lo, LANE)], K)
            cs = jnp.concatenate([ts0[a:a + 1, :] + ts1 for a in range(K)], axis=0)
            ci = jnp.concatenate([ti0[a:a + 1, :] * N_KEYS + ti1 for a in range(K)], axis=0)
            bs, be = _extract_top(cs, K, payload=ci)
            e = jnp.exp(bs - bs[0:1, :])
            g = e / jnp.sum(e, axis=0, keepdims=True)
            r0 = pl.multiple_of(hd * K, K)
            et_ref[pl.ds(r0, K), pl.ds(lo, LANE)] = be
            gt_ref[pl.ds(r0, K), pl.ds(lo, LANE)] = g
            return c2

        lax.fori_loop(0, nchunk, chunk_body, 0)
        return carry

    lax.fori_loop(0, PEER_HEADS, head_body, 0)


def _outproj(h, yl, o, p, *, tm):
    B, Tp, D = h.shape
    nb = Tp // tm
    H = MLA_HEADS
    HK = PEER_HEADS * PEER_TOPK
    full = lambda a: pl.BlockSpec(a.shape, lambda b, i: (0,) * a.ndim)
    tok = lambda w: pl.BlockSpec((None, tm, w), lambda b, i: (b, i, 0))
    head = pl.BlockSpec((None, H, tm, SLOT), lambda b, i: (b, 0, i, 0))
    sel = pl.BlockSpec((HK, tm), lambda b, i: (0, b * nb + i))
    ws = [p["att_g"], p["w_out_lru"], p["w_out_att"], p["ffn_g"], p["peer_wq"], p["subkeys"]]
    return pl.pallas_call(
        functools.partial(_outproj_kernel, tm=tm),
        grid=(B, nb),
        in_specs=[tok(D), tok(LRU_WIDTH), head] + [full(w) for w in ws],
        out_specs=[tok(D), tok(D), sel, sel],
        out_shape=[jax.ShapeDtypeStruct((B, Tp, D), F32), jax.ShapeDtypeStruct((B, Tp, D), BF16),
                   jax.ShapeDtypeStruct((HK, B * Tp), jnp.int32), jax.ShapeDtypeStruct((HK, B * Tp), F32)],
        scratch_shapes=[pltpu.VMEM((tm, PEER_HEADS * 2 * D_SUBKEY), F32), pltpu.VMEM((2, N_KEYS, tm), F32)],
        compiler_params=_params("parallel", "parallel"),
        name="outproj_topk",
    )(h, yl, o, *ws)


def _peer_kernel(hn_ref, et_ref, gt_ref, u_ref, vt_ref, h2_ref, out_ref, act_ref, acc_ref, *, tm, te):
    e = pl.program_id(1)
    HK = PEER_HEADS * PEER_TOPK
    EC = 128

    @pl.when(e == 0)
    def _():
        acc_ref[...] = jnp.zeros(acc_ref.shape, F32)

    act_ref[...] = _dot_nt(u_ref[...], hn_ref[...]).astype(act_ref.dtype)

    def lane_body(jc, carry):
        lo = pl.multiple_of(jc * LANE, LANE)

        def expert_body(ec, c2):
            r0 = pl.multiple_of(ec * EC, EC)
            eio = e * te + r0 + lax.broadcasted_iota(jnp.int32, (EC, LANE), 0)

            def k_body(k8, g):
                k0 = pl.multiple_of(k8 * 8, 8)
                eb = et_ref[pl.ds(k0, 8), pl.ds(lo, LANE)]
                gb = gt_ref[pl.ds(k0, 8), pl.ds(lo, LANE)]
                for r in range(8):
                    g = g + jnp.where(eb[r:r + 1, :] == eio, gb[r:r + 1, :], 0.0)
                return g

            g = lax.fori_loop(0, HK // 8, k_body, jnp.zeros((EC, LANE), F32))
            a = act_ref[pl.ds(r0, EC), pl.ds(lo, LANE)]
            act_ref[pl.ds(r0, EC), pl.ds(lo, LANE)] = _gelu(a) * g
            return c2

        lax.fori_loop(0, te // EC, expert_body, 0)
        return carry

    lax.fori_loop(0, tm // LANE, lane_body, 0)
    acc_ref[...] += _dot(vt_ref[...], act_ref[...].astype(BF16))

    @pl.when(e == pl.num_programs(1) - 1)
    def _():
        out_ref[...] = h2_ref[...] + acc_ref[...].T


def _peer(h2, hn, et, gt, u, vt, *, tm, te):
    N, D = h2.shape
    E = u.shape[0]
    HK = et.shape[0]
    tok = pl.BlockSpec((tm, D), lambda i, e: (i, 0))
    sel = pl.BlockSpec((HK, tm), lambda i, e: (0, i))
    return pl.pallas_call(
        functools.partial(_peer_kernel, tm=tm, te=te),
        grid=(N // tm, E // te),
        in_specs=[tok, sel, sel, pl.BlockSpec((te, D), lambda i, e: (e, 0)),
                  pl.BlockSpec((D, te), lambda i, e: (0, e)), tok],
        out_specs=tok,
        out_shape=jax.ShapeDtypeStruct((N, D), F32),
        scratch_shapes=[pltpu.VMEM((te, tm), F32), pltpu.VMEM((D, tm), F32)],
        compiler_params=_params("parallel", "arbitrary"),
        name="peer_experts",
    )(hn, et, gt, u, vt, h2)


def _slots(w, width):
    lead = w.shape[:-1]
    w = w.reshape(*lead, -1, width)
    w = jnp.pad(w, [(0, 0)] * len(lead) + [(0, 0), (0, SLOT - width)])
    return w.reshape(*lead, -1)


def _block_diag(w):
    n, hd, _ = w.shape
    eye = jnp.eye(n, dtype=w.dtype)
    return (eye[:, None, :, None] * w[:, :, None, :]).reshape(n * hd, n * hd)


def _rope_tables(Tp, n_pad):
    half = QK_ROPE // 2
    pos = (jnp.arange(Tp) - n_pad).astype(F32)
    freq = ROPE_THETA ** (-jnp.arange(half, dtype=F32) / half)
    ang = pos[:, None] * freq[None, :]
    cos, sin = jnp.cos(ang), jnp.sin(ang)
    z = lambda n: jnp.zeros((Tp, n), F32)
    tail = SLOT - QK_DIM
    rc = jnp.concatenate([jnp.ones((Tp, QK_NOPE), F32), cos, cos, jnp.ones((Tp, tail), F32)], axis=1)
    rs1 = jnp.concatenate([z(QK_NOPE), -sin, z(half), z(tail)], axis=1)
    rs2 = jnp.concatenate([z(QK_NOPE), z(half), sin, z(tail)], axis=1)
    return rc, rs1, rs2


def _layer_params(l, mix_norm_g, w_in, conv_w, conv_b, lru_wa, lru_ba, lru_wx, lru_bx, lru_lambda,
                  q_lora_norm_g, w_uq, kv_lora_norm_g, w_ukv, q_head_norm_g, k_head_norm_g, lru_out_norm_g,
                  attn_out_norm_g, w_out, ffn_norm_g, peer_wq, peer_subkeys, peer_u, peer_v):
    H = MLA_HEADS
    row = lambda a: a[None, :].astype(F32)
    wi = w_in[l]
    split4 = 2 * LRU_WIDTH + Q_LORA + KV_LORA
    kr_cols = jnp.pad(wi[:, split4:], ((0, 0), (QK_NOPE, SLOT - QK_DIM)))
    wkv = w_ukv[l].reshape(KV_LORA, H, QK_NOPE + V_DIM)
    w_att = w_out[l][LRU_WIDTH:].reshape(H, V_DIM, -1)
    return {
        "mix_g": row(mix_norm_g[l]),
        "w_in": jnp.concatenate([wi[:, :split4], kr_cols], axis=1).astype(BF16),
        "q_g": row(q_lora_norm_g[l]),
        "w_uq": _slots(w_uq[l], QK_DIM).astype(BF16),
        "kv_g": row(kv_lora_norm_g[l]),
        "w_uk": _slots(wkv[:, :, :QK_NOPE].reshape(KV_LORA, H * QK_NOPE), QK_NOPE).astype(BF16),
        "w_uv": _slots(wkv[:, :, QK_NOPE:].reshape(KV_LORA, H * V_DIM), V_DIM).astype(BF16),
        "qh_g": _slots(row(q_head_norm_g[l]), QK_DIM),
        "kh_g": _slots(row(k_head_norm_g[l]), QK_DIM),
        "conv_w": conv_w[l],
        "conv_b": row(conv_b[l]),
        "wa": _block_diag(lru_wa[l]).astype(BF16),
        "ba": row(lru_ba[l].reshape(-1)),
        "wx": _block_diag(lru_wx[l]).astype(BF16),
        "bx": row(lru_bx[l].reshape(-1)),
        "lam": row(lru_lambda[l]),
        "lru_og": row(lru_out_norm_g[l]),
        "att_g": _slots(row(attn_out_norm_g[l]), V_DIM).reshape(1, H, SLOT).transpose(1, 0, 2),
        "w_out_lru": w_out[l][:LRU_WIDTH].astype(BF16),
        "w_out_att": jnp.pad(w_att, ((0, 0), (0, SLOT - V_DIM), (0, 0))).astype(BF16),
        "ffn_g": row(ffn_norm_g[l]),
        "peer_wq": peer_wq[l].astype(BF16),
        "subkeys": peer_subkeys[l].astype(BF16),
        "u": peer_u[l].astype(BF16),
        "vt": peer_v[l].T.astype(BF16),
    }


def _tiles(B, Tp):
    nblk = Tp // ATTN_BLOCK
    per_tok = ATTN_BLOCK * max(d for d in range(1, 6) if nblk % d == 0)
    n = B * nblk
    flat = ATTN_BLOCK * max(d for d in range(1, 11) if n % d == 0)
    return per_tok, flat


def kernel(x, meta_tokens, mix_norm_g, w_in, conv_w, conv_b, lru_wa, lru_ba, lru_wx, lru_bx, lru_lambda,
           q_lora_norm_g, w_uq, kv_lora_norm_g, w_ukv, q_head_norm_g, k_head_norm_g, lru_out_norm_g,
           attn_out_norm_g, w_out, ffn_norm_g, peer_wq, peer_subkeys, peer_u, peer_v):
    B, S, D = x.shape
    depth = w_in.shape[0]
    T = N_META + S
    n_pad = (-T) % ATTN_BLOCK
    Tp = T + n_pad
    tm, tflat = _tiles(B, Tp)
    meta = jnp.broadcast_to(meta_tokens.astype(x.dtype)[None], (B, N_META, D))
    h = jnp.concatenate([jnp.zeros((B, n_pad, D), x.dtype), meta, x], axis=1)
    rope_tabs = _rope_tables(Tp, n_pad)
    weights = (mix_norm_g, w_in, conv_w, conv_b, lru_wa, lru_ba, lru_wx, lru_bx, lru_lambda, q_lora_norm_g,
               w_uq, kv_lora_norm_g, w_ukv, q_head_norm_g, k_head_norm_g, lru_out_norm_g, attn_out_norm_g,
               w_out, ffn_norm_g, peer_wq, peer_subkeys, peer_u, peer_v)
    for l in range(depth):
        p = _layer_params(l, *weights)
        xl, gg, q, k, v = _inproj(h, p, rope_tabs, tm=tm, n_pad=n_pad)
        yl = _lru(xl, gg, p, tt=tm, n_pad=n_pad)
        o = _attention(q, k, v, tq=tm // 2 if (tm // 2) % 8 == 0 else tm, tk=tm)
        h2, hn, et, gt = _outproj(h, yl, o, p, tm=tm)
        h = _peer(h2.reshape(B * Tp, D), hn.reshape(B * Tp, D), et, gt, p["u"], p["vt"],
                  tm=tflat, te=1024).reshape(B, Tp, D)
    return h[:, n_pad:][:, N_META:]
```

```python
import functools
import math

import jax
import jax.numpy as jnp
from jax import lax
from jax.experimental import pallas as pl
from jax.experimental.pallas import tpu as pltpu

N_META = 16
LRU_WIDTH = 512
LRU_HEADS = 8
LRU_HEAD_DIM = LRU_WIDTH // LRU_HEADS
CONV_WIDTH = 4
LRU_C = 8.0
MLA_HEADS = 8
Q_LORA = 384
KV_LORA = 256
QK_NOPE = 64
QK_ROPE = 32
QK_DIM = QK_NOPE + QK_ROPE
V_DIM = 64
ROPE_THETA = 10000.0
ATTN_BLOCK = 128
PEER_HEADS = 8
N_KEYS = 128
PEER_TOPK = 16
D_SUBKEY = 128
EPS = 1e-6
NEG_INF = -1e30

LANE = 128
SLOT = LANE
MASK_LANE = QK_DIM
ONES_LANE = V_DIM
M_INIT = -3.0e38
VMEM_LIMIT = 56 * 1024 * 1024

F32 = jnp.float32
BF16 = jnp.bfloat16
INV_SQRT2 = 1.0 / math.sqrt(2.0)


def _gelu(x):
    return 0.5 * x * (1.0 + lax.erf(x * INV_SQRT2))


def _rms(x, g, n=None):
    n = x.shape[-1] if n is None else n
    ms = jnp.sum(x * x, axis=-1, keepdims=True) * (1.0 / n)
    return x * lax.rsqrt(ms + EPS) * g


def _dot(a, b):
    return jnp.dot(a, b, preferred_element_type=F32)


def _dot_nt(a, b):
    return lax.dot_general(a, b, (((1,), (1,)), ((), ())), preferred_element_type=F32)


def _params(*sem):
    return pltpu.CompilerParams(dimension_semantics=sem, vmem_limit_bytes=VMEM_LIMIT)


def _inproj_kernel(h_ref, g_ref, win_ref, qg_ref, wuq_ref, kvg_ref, wuk_ref, wuv_ref, qhg_ref, khg_ref,
                   rc_ref, rs1_ref, rs2_ref, xl_ref, gg_ref, q_ref, k_ref, v_ref, *, tm, n_pad):
    h = h_ref[...]
    hn = _rms(h, g_ref[...])
    z = _dot(hn.astype(BF16), win_ref[...])
    o1 = LRU_WIDTH
    o2 = 2 * LRU_WIDTH
    o3 = o2 + Q_LORA
    o4 = o3 + KV_LORA
    xl_ref[...] = z[:, :o1]
    gg_ref[...] = _gelu(z[:, o1:o2])
    cqn = _rms(z[:, o2:o3], qg_ref[...]).astype(BF16)
    ckvn = _rms(z[:, o3:o4], kvg_ref[...]).astype(BF16)
    kr = z[:, o4:o4 + SLOT]
    q = _dot(cqn, wuq_ref[...])
    kn = _dot(ckvn, wuk_ref[...])
    v = _dot(ckvn, wuv_ref[...])
    rc = rc_ref[...]
    rs1 = rs1_ref[...]
    rs2 = rs2_ref[...]
    half = QK_ROPE // 2
    lane = lax.broadcasted_iota(jnp.int32, (1, SLOT), 1)
    mask_lane = (lane == MASK_LANE).astype(F32)
    ones_lane = (lane == ONES_LANE).astype(F32)
    row = pl.program_id(1) * tm + lax.broadcasted_iota(jnp.int32, (tm, 1), 0)
    kbias = jnp.where(row < n_pad, NEG_INF, 0.0) * mask_lane
    scale = 1.0 / math.sqrt(QK_DIM)

    def rope(x):
        return x * rc + pltpu.roll(x, SLOT - half, 1) * rs1 + pltpu.roll(x, half, 1) * rs2

    for hd in range(MLA_HEADS):
        sl = slice(hd * SLOT, (hd + 1) * SLOT)
        qs = rope(_rms(q[:, sl], qhg_ref[...], QK_DIM))
        q_ref[hd] = (qs * scale + mask_lane).astype(BF16)
        ks = rope(_rms(kn[:, sl] + kr, khg_ref[...], QK_DIM))
        k_ref[hd] = (ks + kbias).astype(BF16)
        v_ref[hd] = (v[:, sl] + ones_lane).astype(BF16)


def _inproj(h, p, rope_tabs, *, tm, n_pad):
    B, Tp, D = h.shape
    nb = Tp // tm
    H = MLA_HEADS
    full = lambda a: pl.BlockSpec(a.shape, lambda b, i: (0,) * a.ndim)
    tok = lambda w: pl.BlockSpec((None, tm, w), lambda b, i: (b, i, 0))
    head = pl.BlockSpec((None, H, tm, SLOT), lambda b, i: (b, 0, i, 0))
    tab = pl.BlockSpec((tm, SLOT), lambda b, i: (i, 0))
    ws = [p["mix_g"], p["w_in"], p["q_g"], p["w_uq"], p["kv_g"], p["w_uk"], p["w_uv"], p["qh_g"], p["kh_g"]]
    return pl.pallas_call(
        functools.partial(_inproj_kernel, tm=tm, n_pad=n_pad),
        grid=(B, nb),
        in_specs=[tok(D)] + [full(w) for w in ws] + [tab, tab, tab],
        out_specs=[tok(LRU_WIDTH), tok(LRU_WIDTH), head, head, head],
        out_shape=[jax.ShapeDtypeStruct((B, Tp, LRU_WIDTH), F32), jax.ShapeDtypeStruct((B, Tp, LRU_WIDTH), F32)]
        + [jax.ShapeDtypeStruct((B, H, Tp, SLOT), BF16)] * 3,
        compiler_params=_params("parallel", "parallel"),
        name="inproj",
    )(h, *ws, *rope_tabs)


def _lru_kernel(xl_ref, gg_ref, cw_ref, cb_ref, wa_ref, ba_ref, wx_ref, bx_ref, lam_ref, og_ref, y_ref,
                ext_ref, hst_ref, *, tt, n_pad):
    t = pl.program_id(1)
    HALO = 8

    @pl.when(t == 0)
    def _():
        ext_ref[0:HALO, :] = jnp.zeros((HALO, LRU_WIDTH), F32)
        hst_ref[...] = jnp.zeros((HALO, LRU_WIDTH), F32)

    row = t * tt + lax.broadcasted_iota(jnp.int32, (tt, 1), 0)
    real = row >= n_pad
    ext_ref[HALO:HALO + tt, :] = jnp.where(real, xl_ref[...], 0.0)
    xc = cb_ref[...]
    for k in range(CONV_WIDTH):
        off = HALO - (CONV_WIDTH - 1) + k
        xc = xc + ext_ref[off:off + tt, :] * cw_ref[k:k + 1, :]
    xcb = xc.astype(BF16)
    r = jax.nn.sigmoid(_dot(xcb, wa_ref[...]) + ba_ref[...])
    i = jax.nn.sigmoid(_dot(xcb, wx_ref[...]) + bx_ref[...])
    nl = -lam_ref[...]
    softplus = jnp.maximum(nl, 0.0) + jnp.log1p(jnp.exp(-jnp.abs(nl)))
    log_a = -LRU_C * r * softplus
    a = jnp.exp(log_a)
    mult = jnp.sqrt(1.0 - jnp.exp(2.0 * log_a))
    b = jnp.where(real, mult * (i * xc), 0.0)
    ridx = lax.broadcasted_iota(jnp.int32, (tt, 1), 0)
    s = 1
    while s < tt:
        keep = ridx >= s
        a_sh = jnp.where(keep, pltpu.roll(a, s, 0), 1.0)
        b_sh = jnp.where(keep, pltpu.roll(b, s, 0), 0.0)
        b = a * b_sh + b
        a = a * a_sh
        s *= 2
    hprev = hst_ref[HALO - 1:HALO, :]
    hs = a * hprev + b
    hst_ref[...] = hs[tt - HALO:tt, :]
    ext_ref[0:HALO, :] = ext_ref[tt:tt + HALO, :]
    y = hs * gg_ref[...]
    y_ref[...] = _rms(y, og_ref[...]).astype(BF16)


def _lru(xl, gg, p, *, tt, n_pad):
    B, Tp, W = xl.shape
    full = lambda a: pl.BlockSpec(a.shape, lambda b, i: (0,) * a.ndim)
    tok = pl.BlockSpec((None, tt, W), lambda b, i: (b, i, 0))
    ws = [p["conv_w"], p["conv_b"], p["wa"], p["ba"], p["wx"], p["bx"], p["lam"], p["lru_og"]]
    return pl.pallas_call(
        functools.partial(_lru_kernel, tt=tt, n_pad=n_pad),
        grid=(B, Tp // tt),
        in_specs=[tok, tok] + [full(w) for w in ws],
        out_specs=tok,
        out_shape=jax.ShapeDtypeStruct((B, Tp, W), BF16),
        scratch_shapes=[pltpu.VMEM((tt + 8, W), F32), pltpu.VMEM((8, W), F32)],
        compiler_params=_params("parallel", "arbitrary"),
        name="rglru",
    )(xl, gg, *ws)


def _attn_kernel(q_ref, k_ref, v_ref, o_ref, *, tq, tk):
    qi = pl.program_id(2)
    q = q_ref[...]
    kd = (qi * tq) // tk

    def step(kb, carry, diagonal):
        m, acc = carry
        start = pl.multiple_of(kb * tk, tk)
        s = _dot_nt(q, k_ref[pl.ds(start, tk), :])
        if diagonal:
            qpos = qi * tq + lax.broadcasted_iota(jnp.int32, (tq, tk), 0)
            kpos = kb * tk + lax.broadcasted_iota(jnp.int32, (tq, tk), 1)
            s = jnp.where(kpos <= qpos, s, NEG_INF)
        m_new = jnp.maximum(m, jnp.max(s, axis=-1, keepdims=True))
        alpha = jnp.exp(m - m_new)
        pr = jnp.exp(s - m_new).astype(BF16)
        acc = acc * alpha + _dot(pr, v_ref[pl.ds(start, tk), :])
        return m_new, acc

    init = (jnp.full((tq, 1), M_INIT, F32), jnp.zeros((tq, SLOT), F32))
    m, acc = lax.fori_loop(0, kd, lambda kb, c: step(kb, c, False), init)
    m, acc = step(kd, (m, acc), True)
    lane = lax.broadcasted_iota(jnp.int32, (1, SLOT), 1)
    o = acc / acc[:, ONES_LANE:ONES_LANE + 1]
    o_ref[...] = jnp.where(lane < V_DIM, o, 0.0).astype(BF16)


def _attention(q, k, v, *, tq, tk):
    B, H, Tp, _ = q.shape
    qspec = pl.BlockSpec((None, None, tq, SLOT), lambda b, h, i: (b, h, i, 0))
    kvspec = pl.BlockSpec((None, None, Tp, SLOT), lambda b, h, i: (b, h, 0, 0))
    return pl.pallas_call(
        functools.partial(_attn_kernel, tq=tq, tk=tk),
        grid=(B, H, Tp // tq),
        in_specs=[qspec, kvspec, kvspec],
        out_specs=qspec,
        out_shape=jax.ShapeDtypeStruct((B, H, Tp, SLOT), BF16),
        compiler_params=_params("parallel", "parallel", "arbitrary"),
        name="attention",
    )(q, k, v)


def _extract_top(x, n, payload=None):
    R, L = x.shape
    pos = lax.broadcasted_iota(jnp.int32, (R, L), 0)
    out_row = lax.broadcasted_iota(jnp.int32, (n, L), 0)
    vals = jnp.zeros((n, L), F32)
    idxs = jnp.zeros((n, L), jnp.int32)
    for it in range(n):
        m = jnp.max(x, axis=0, keepdims=True)
        idx = jnp.min(jnp.where(x == m, pos, R), axis=0, keepdims=True)
        hit = pos == idx
        if payload is None:
            sel = idx
        else:
            sel = jnp.sum(jnp.where(hit, payload, 0), axis=0, keepdims=True)
        vals = jnp.where(out_row == it, m, vals)
        idxs = jnp.where(out_row == it, sel, idxs)
        x = jnp.where(hit, -jnp.inf, x)
    return vals, idxs


def _outproj_kernel(h_ref, yl_ref, o_ref, ag_ref, wol_ref, woa_ref, fg_ref, wq_ref, sk_ref,
                    h2_ref, hn_ref, et_ref, gt_ref, qp_ref, s_ref, *, tm):
    H = MLA_HEADS
    ssq = jnp.zeros((tm, 1), F32)
    for hd in range(H):
        oh = o_ref[hd].astype(F32)
        ssq = ssq + jnp.sum(oh * oh, axis=-1, keepdims=True)
    inv = lax.rsqrt(ssq * (1.0 / (H * V_DIM)) + EPS)
    mix = _dot(yl_ref[...], wol_ref[...])
    for hd in range(H):
        an = (o_ref[hd].astype(F32) * inv * ag_ref[hd]).astype(BF16)
        mix = mix + _dot(an, woa_ref[hd])
    h2 = h_ref[...] + mix
    h2_ref[...] = h2
    hn = _rms(h2, fg_ref[...]).astype(BF16)
    hn_ref[...] = hn
    qp_ref[...] = _dot(hn, wq_ref[...])
    K = PEER_TOPK
    nchunk = tm // LANE

    def head_body(hd, carry):
        for pp in range(2):
            off = pl.multiple_of((hd * 2 + pp) * D_SUBKEY, D_SUBKEY)
            qs = qp_ref[:, pl.ds(off, D_SUBKEY)].astype(BF16)
            s_ref[pp] = _dot_nt(sk_ref[pp], qs)

        def chunk_body(j, c2):
            lo = pl.multiple_of(j * LANE, LANE)
            ts0, ti0 = _extract_top(s_ref[0, :, pl.ds(lo, LANE)], K)
            ts1, ti1 = _extract_top(s_ref[1, :, pl.ds(lo, LANE)], K)
            cs = jnp.concatenate([ts0[a:a + 1, :] + ts1 for a in range(K)], axis=0)
            ci = jnp.concatenate([ti0[a:a + 1, :] * N_KEYS + ti1 for a in range(K)], axis=0)
            bs, be = _extract_top(cs, K, payload=ci)
            e = jnp.exp(bs - bs[0:1, :])
            g = e / jnp.sum(e, axis=0, keepdims=True)
            r0 = pl.multiple_of(hd * K, K)
            et_ref[pl.ds(r0, K), pl.ds(lo, LANE)] = be
            gt_ref[pl.ds(r0, K), pl.ds(lo, LANE)] = g
            return c2

        lax.fori_loop(0, nchunk, chunk_body, 0)
        return carry

    lax.fori_loop(0, PEER_HEADS, head_body, 0)


def _outproj(h, yl, o, p, *, tm):
    B, Tp, D = h.shape
    nb = Tp // tm
    H = MLA_HEADS
    HK = PEER_HEADS * PEER_TOPK
    full = lambda a: pl.BlockSpec(a.shape, lambda b, i: (0,) * a.ndim)
    tok = lambda w: pl.BlockSpec((None, tm, w), lambda b, i: (b, i, 0))
    head = pl.BlockSpec((None, H, tm, SLOT), lambda b, i: (b, 0, i, 0))
    sel = pl.BlockSpec((HK, tm), lambda b, i: (0, b * nb + i))
    ws = [p["att_g"], p["w_out_lru"], p["w_out_att"], p["ffn_g"], p["peer_wq"], p["subkeys"]]
    return pl.pallas_call(
        functools.partial(_outproj_kernel, tm=tm),
        grid=(B, nb),
        in_specs=[tok(D), tok(LRU_WIDTH), head] + [full(w) for w in ws],
        out_specs=[tok(D), tok(D), sel, sel],
        out_shape=[jax.ShapeDtypeStruct((B, Tp, D), F32), jax.ShapeDtypeStruct((B, Tp, D), BF16),
                   jax.ShapeDtypeStruct((HK, B * Tp), jnp.int32), jax.ShapeDtypeStruct((HK, B * Tp), F32)],
        scratch_shapes=[pltpu.VMEM((tm, PEER_HEADS * 2 * D_SUBKEY), F32), pltpu.VMEM((2, N_KEYS, tm), F32)],
        compiler_params=_params("parallel", "parallel"),
        name="outproj_topk",
    )(h, yl, o, *ws)


def _build_slot_tables(c, et_ref, gt_ref, tabi_ref, tabg_ref, eperm_ref, gov_ref, rcnt_ref, *, rcap):
    HK = PEER_HEADS * PEER_TOPK
    NV = HK // 8
    lo = pl.multiple_of(c * LANE, LANE)
    ex = et_ref[:, pl.ds(lo, LANE)]
    gx = gt_ref[:, pl.ds(lo, LANE)]
    ik = ex >> 7
    dk = (ik + (ex & (N_KEYS - 1))) & (N_KEYS - 1)
    dv = [dk[v * 8:(v + 1) * 8, :] for v in range(NV)]
    rank = [jnp.zeros((8, LANE), jnp.int32) for _ in range(NV)]
    sub = lax.broadcasted_iota(jnp.int32, (8, LANE), 0)
    for kp in range(HK - 1):
        v0, r0 = divmod(kp, 8)
        row = dv[v0][r0:r0 + 1, :]
        if r0 < 7:
            rank[v0] = rank[v0] + jnp.where(sub > r0, jnp.where(dv[v0] == row, 1, 0), 0)
        for v in range(v0 + 1, NV):
            rank[v] = rank[v] + jnp.where(dv[v] == row, 1, 0)
    rank = jnp.concatenate(rank, axis=0)
    key = rank * N_KEYS + dk
    rcnt_ref[c] = jnp.max(rank) + 1
    eperm_ref[:, pl.ds(lo, LANE)] = ik * N_KEYS + dk
    gov_ref[:, pl.ds(lo, LANE)] = jnp.where(rank >= rcap, gx, 0.0)
    dio = lax.broadcasted_iota(jnp.int32, (N_KEYS, LANE), 0)

    def round_body(r, carry):
        want = dio + r * N_KEYS
        ti = jnp.full((N_KEYS, LANE), -1, jnp.int32)
        tg = jnp.zeros((N_KEYS, LANE), F32)
        for k in range(HK):
            hit = key[k:k + 1, :] == want
            ti = jnp.where(hit, ik[k:k + 1, :], ti)
            tg = jnp.where(hit, gx[k:k + 1, :], tg)
        r0 = pl.multiple_of(r * N_KEYS, N_KEYS)
        tabi_ref[pl.ds(r0, N_KEYS), pl.ds(lo, LANE)] = ti
        tabg_ref[pl.ds(r0, N_KEYS), pl.ds(lo, LANE)] = tg
        return carry

    lax.fori_loop(0, jnp.minimum(rcnt_ref[c], rcap), round_body, 0)


def _peer_kernel(hn_ref, et_ref, gt_ref, u_ref, vt_ref, h2_ref, out_ref, act_ref, acc_ref,
                 tabi_ref, tabg_ref, eperm_ref, gov_ref, rcnt_ref, *, tm, te, rcap):
    e = pl.program_id(1)
    HK = PEER_HEADS * PEER_TOPK
    nchunk = tm // LANE

    @pl.when(e == 0)
    def _():
        acc_ref[...] = jnp.zeros(acc_ref.shape, F32)

        def build(c, carry):
            _build_slot_tables(c, et_ref, gt_ref, tabi_ref, tabg_ref, eperm_ref, gov_ref, rcnt_ref, rcap=rcap)
            return carry

        lax.fori_loop(0, nchunk, build, 0)

    act_ref[...] = _dot_nt(u_ref[...], hn_ref[...])

    def lane_body(c, carry):
        lo = pl.multiple_of(c * LANE, LANE)
        nround = rcnt_ref[c]
        ntab = jnp.minimum(nround, rcap)
        nover = jnp.where(nround > rcap, HK // 8, 0)

        def tile_body(it, c2):
            i_abs = e * (te // N_KEYS) + it
            r0 = pl.multiple_of(it * N_KEYS, N_KEYS)

            def round_body(r, g):
                t0 = pl.multiple_of(r * N_KEYS, N_KEYS)
                ti = tabi_ref[pl.ds(t0, N_KEYS), pl.ds(lo, LANE)]
                tg = tabg_ref[pl.ds(t0, N_KEYS), pl.ds(lo, LANE)]
                return g + jnp.where(ti == i_abs, tg, 0.0)

            g = lax.fori_loop(0, ntab, round_body, jnp.zeros((N_KEYS, LANE), F32))
            eio = i_abs * N_KEYS + lax.broadcasted_iota(jnp.int32, (N_KEYS, LANE), 0)

            def over_body(k8, g):
                k0 = pl.multiple_of(k8 * 8, 8)
                eb = eperm_ref[pl.ds(k0, 8), pl.ds(lo, LANE)]
                gb = gov_ref[pl.ds(k0, 8), pl.ds(lo, LANE)]
                for r in range(8):
                    g = g + jnp.where(eb[r:r + 1, :] == eio, gb[r:r + 1, :], 0.0)
                return g

            g = lax.fori_loop(0, nover, over_body, g)
            a = act_ref[pl.ds(r0, N_KEYS), pl.ds(lo, LANE)]
            act_ref[pl.ds(r0, N_KEYS), pl.ds(lo, LANE)] = _gelu(a) * g
            return c2

        lax.fori_loop(0, te // N_KEYS, tile_body, 0)
        return carry

    lax.fori_loop(0, nchunk, lane_body, 0)
    acc_ref[...] += _dot(vt_ref[...], act_ref[...].astype(BF16))

    @pl.when(e == pl.num_programs(1) - 1)
    def _():
        out_ref[...] = h2_ref[...] + acc_ref[...].T


def _peer(h2, hn, et, gt, u, vt, *, tm, te, rcap=16):
    N, D = h2.shape
    E = u.shape[0]
    HK = et.shape[0]
    tok = pl.BlockSpec((tm, D), lambda i, e: (i, 0))
    sel = pl.BlockSpec((HK, tm), lambda i, e: (0, i))
    return pl.pallas_call(
        functools.partial(_peer_kernel, tm=tm, te=te, rcap=rcap),
        grid=(N // tm, E // te),
        in_specs=[tok, sel, sel, pl.BlockSpec((te, D), lambda i, e: (e, 0)),
                  pl.BlockSpec((D, te), lambda i, e: (0, e)), tok],
        out_specs=tok,
        out_shape=jax.ShapeDtypeStruct((N, D), F32),
        scratch_shapes=[pltpu.VMEM((te, tm), F32), pltpu.VMEM((D, tm), F32),
                        pltpu.VMEM((rcap * N_KEYS, tm), jnp.int32), pltpu.VMEM((rcap * N_KEYS, tm), F32),
                        pltpu.VMEM((HK, tm), jnp.int32), pltpu.VMEM((HK, tm), F32),
                        pltpu.SMEM((tm // LANE,), jnp.int32)],
        compiler_params=_params("parallel", "arbitrary"),
        name="peer_experts",
    )(hn, et, gt, u, vt, h2)


def _diag_experts(w):
    D = w.shape[-1]
    i = jnp.arange(N_KEYS)[:, None]
    d = jnp.arange(N_KEYS)[None, :]
    j = (d - i) % N_KEYS
    return jnp.take_along_axis(w.reshape(N_KEYS, N_KEYS, D), j[:, :, None], axis=1).reshape(N_KEYS * N_KEYS, D)


def _slots(w, width):
    lead = w.shape[:-1]
    w = w.reshape(*lead, -1, width)
    w = jnp.pad(w, [(0, 0)] * len(lead) + [(0, 0), (0, SLOT - width)])
    return w.reshape(*lead, -1)


def _block_diag(w):
    n, hd, _ = w.shape
    eye = jnp.eye(n, dtype=w.dtype)
    return (eye[:, None, :, None] * w[:, :, None, :]).reshape(n * hd, n * hd)


def _rope_tables(Tp, n_pad):
    half = QK_ROPE // 2
    pos = (jnp.arange(Tp) - n_pad).astype(F32)
    freq = ROPE_THETA ** (-jnp.arange(half, dtype=F32) / half)
    ang = pos[:, None] * freq[None, :]
    cos, sin = jnp.cos(ang), jnp.sin(ang)
    z = lambda n: jnp.zeros((Tp, n), F32)
    tail = SLOT - QK_DIM
    rc = jnp.concatenate([jnp.ones((Tp, QK_NOPE), F32), cos, cos, jnp.ones((Tp, tail), F32)], axis=1)
    rs1 = jnp.concatenate([z(QK_NOPE), -sin, z(half), z(tail)], axis=1)
    rs2 = jnp.concatenate([z(QK_NOPE), z(half), sin, z(tail)], axis=1)
    return rc, rs1, rs2


def _layer_params(l, mix_norm_g, w_in, conv_w, conv_b, lru_wa, lru_ba, lru_wx, lru_bx, lru_lambda,
                  q_lora_norm_g, w_uq, kv_lora_norm_g, w_ukv, q_head_norm_g, k_head_norm_g, lru_out_norm_g,
                  attn_out_norm_g, w_out, ffn_norm_g, peer_wq, peer_subkeys, peer_u, peer_v):
    H = MLA_HEADS
    row = lambda a: a[None, :].astype(F32)
    wi = w_in[l]
    split4 = 2 * LRU_WIDTH + Q_LORA + KV_LORA
    kr_cols = jnp.pad(wi[:, split4:], ((0, 0), (QK_NOPE, SLOT - QK_DIM)))
    wkv = w_ukv[l].reshape(KV_LORA, H, QK_NOPE + V_DIM)
    w_att = w_out[l][LRU_WIDTH:].reshape(H, V_DIM, -1)
    return {
        "mix_g": row(mix_norm_g[l]),
        "w_in": jnp.concatenate([wi[:, :split4], kr_cols], axis=1).astype(BF16),
        "q_g": row(q_lora_norm_g[l]),
        "w_uq": _slots(w_uq[l], QK_DIM).astype(BF16),
        "kv_g": row(kv_lora_norm_g[l]),
        "w_uk": _slots(wkv[:, :, :QK_NOPE].reshape(KV_LORA, H * QK_NOPE), QK_NOPE).astype(BF16),
        "w_uv": _slots(wkv[:, :, QK_NOPE:].reshape(KV_LORA, H * V_DIM), V_DIM).astype(BF16),
        "qh_g": _slots(row(q_head_norm_g[l]), QK_DIM),
        "kh_g": _slots(row(k_head_norm_g[l]), QK_DIM),
        "conv_w": conv_w[l],
        "conv_b": row(conv_b[l]),
        "wa": _block_diag(lru_wa[l]).astype(BF16),
        "ba": row(lru_ba[l].reshape(-1)),
        "wx": _block_diag(lru_wx[l]).astype(BF16),
        "bx": row(lru_bx[l].reshape(-1)),
        "lam": row(lru_lambda[l]),
        "lru_og": row(lru_out_norm_g[l]),
        "att_g": _slots(row(attn_out_norm_g[l]), V_DIM).reshape(1, H, SLOT).transpose(1, 0, 2),
        "w_out_lru": w_out[l][:LRU_WIDTH].astype(BF16),
        "w_out_att": jnp.pad(w_att, ((0, 0), (0, SLOT - V_DIM), (0, 0))).astype(BF16),
        "ffn_g": row(ffn_norm_g[l]),
        "peer_wq": peer_wq[l].astype(BF16),
        "subkeys": peer_subkeys[l].astype(BF16),
        "u": _diag_experts(peer_u[l]).astype(BF16),
        "vt": _diag_experts(peer_v[l]).T.astype(BF16),
    }


def _tiles(B, Tp):
    nblk = Tp // ATTN_BLOCK
    per_tok = ATTN_BLOCK * max(d for d in range(1, 6) if nblk % d == 0)
    n = B * nblk
    flat = ATTN_BLOCK * max(d for d in range(1, 6) if n % d == 0)
    return per_tok, flat


def kernel(x, meta_tokens, mix_norm_g, w_in, conv_w, conv_b, lru_wa, lru_ba, lru_wx, lru_bx, lru_lambda,
           q_lora_norm_g, w_uq, kv_lora_norm_g, w_ukv, q_head_norm_g, k_head_norm_g, lru_out_norm_g,
           attn_out_norm_g, w_out, ffn_norm_g, peer_wq, peer_subkeys, peer_u, peer_v):
    B, S, D = x.shape
    depth = w_in.shape[0]
    T = N_META + S
    n_pad = (-T) % ATTN_BLOCK
    Tp = T + n_pad
    tm, tflat = _tiles(B, Tp)
    meta = jnp.broadcast_to(meta_tokens.astype(x.dtype)[None], (B, N_META, D))
    h = jnp.concatenate([jnp.zeros((B, n_pad, D), x.dtype), meta, x], axis=1)
    rope_tabs = _rope_tables(Tp, n_pad)
    weights = (mix_norm_g, w_in, conv_w, conv_b, lru_wa, lru_ba, lru_wx, lru_bx, lru_lambda, q_lora_norm_g,
               w_uq, kv_lora_norm_g, w_ukv, q_head_norm_g, k_head_norm_g, lru_out_norm_g, attn_out_norm_g,
               w_out, ffn_norm_g, peer_wq, peer_subkeys, peer_u, peer_v)
    for l in range(depth):
        p = _layer_params(l, *weights)
        xl, gg, q, k, v = _inproj(h, p, rope_tabs, tm=tm, n_pad=n_pad)
        yl = _lru(xl, gg, p, tt=tm, n_pad=n_pad)
        o = _attention(q, k, v, tq=tm // 2 if (tm // 2) % 8 == 0 else tm, tk=tm)
        h2, hn, et, gt = _outproj(h, yl, o, p, tm=tm)
        h = _peer(h2.reshape(B * Tp, D), hn.reshape(B * Tp, D), et, gt, p["u"], p["vt"],
                  tm=tflat, te=1024).reshape(B, Tp, D)
    return h[:, n_pad:][:, N_META:]
```

```python
import functools
import math

import jax
import jax.numpy as jnp
from jax import lax
from jax.experimental import pallas as pl
from jax.experimental.pallas import tpu as pltpu

N_META = 16
LRU_WIDTH = 512
LRU_HEADS = 8
LRU_HEAD_DIM = LRU_WIDTH // LRU_HEADS
CONV_WIDTH = 4
LRU_C = 8.0
MLA_HEADS = 8
Q_LORA = 384
KV_LORA = 256
QK_NOPE = 64
QK_ROPE = 32
QK_DIM = QK_NOPE + QK_ROPE
V_DIM = 64
ROPE_THETA = 10000.0
ATTN_BLOCK = 128
PEER_HEADS = 8
N_KEYS = 128
PEER_TOPK = 16
D_SUBKEY = 128
EPS = 1e-6
NEG_INF = -1e30

LANE = 128
SLOT = LANE
MASK_LANE = QK_DIM
ONES_LANE = V_DIM
M_INIT = -3.0e38
VMEM_LIMIT = 56 * 1024 * 1024

F32 = jnp.float32
BF16 = jnp.bfloat16
INV_SQRT2 = 1.0 / math.sqrt(2.0)


def _gelu(x):
    return 0.5 * x * (1.0 + lax.erf(x * INV_SQRT2))


def _rms(x, g, n=None):
    n = x.shape[-1] if n is None else n
    ms = jnp.sum(x * x, axis=-1, keepdims=True) * (1.0 / n)
    return x * lax.rsqrt(ms + EPS) * g


def _dot(a, b):
    return jnp.dot(a, b, preferred_element_type=F32)


def _dot_nt(a, b):
    return lax.dot_general(a, b, (((1,), (1,)), ((), ())), preferred_element_type=F32)


def _params(*sem):
    return pltpu.CompilerParams(dimension_semantics=sem, vmem_limit_bytes=VMEM_LIMIT)


def _inproj_kernel(h_ref, g_ref, win_ref, qg_ref, wuq_ref, kvg_ref, wuk_ref, wuv_ref, qhg_ref, khg_ref,
                   rc_ref, rs1_ref, rs2_ref, xl_ref, gg_ref, q_ref, k_ref, v_ref, *, tm, n_pad):
    h = h_ref[...]
    hn = _rms(h, g_ref[...])
    z = _dot(hn.astype(BF16), win_ref[...])
    o1 = LRU_WIDTH
    o2 = 2 * LRU_WIDTH
    o3 = o2 + Q_LORA
    o4 = o3 + KV_LORA
    xl_ref[...] = z[:, :o1]
    gg_ref[...] = _gelu(z[:, o1:o2])
    cqn = _rms(z[:, o2:o3], qg_ref[...]).astype(BF16)
    ckvn = _rms(z[:, o3:o4], kvg_ref[...]).astype(BF16)
    kr = z[:, o4:o4 + SLOT]
    q = _dot(cqn, wuq_ref[...])
    kn = _dot(ckvn, wuk_ref[...])
    v = _dot(ckvn, wuv_ref[...])
    rc = rc_ref[...]
    rs1 = rs1_ref[...]
    rs2 = rs2_ref[...]
    half = QK_ROPE // 2
    lane = lax.broadcasted_iota(jnp.int32, (1, SLOT), 1)
    mask_lane = (lane == MASK_LANE).astype(F32)
    ones_lane = (lane == ONES_LANE).astype(F32)
    row = pl.program_id(1) * tm + lax.broadcasted_iota(jnp.int32, (tm, 1), 0)
    kbias = jnp.where(row < n_pad, NEG_INF, 0.0) * mask_lane
    scale = 1.0 / math.sqrt(QK_DIM)

    def rope(x):
        return x * rc + pltpu.roll(x, SLOT - half, 1) * rs1 + pltpu.roll(x, half, 1) * rs2

    for hd in range(MLA_HEADS):
        sl = slice(hd * SLOT, (hd + 1) * SLOT)
        qs = rope(_rms(q[:, sl], qhg_ref[...], QK_DIM))
        q_ref[hd] = (qs * scale + mask_lane).astype(BF16)
        ks = rope(_rms(kn[:, sl] + kr, khg_ref[...], QK_DIM))
        k_ref[hd] = (ks + kbias).T.astype(BF16)
        v_ref[hd] = (v[:, sl] + ones_lane).astype(BF16)


def _inproj(h, p, rope_tabs, *, tm, n_pad):
    B, Tp, D = h.shape
    nb = Tp // tm
    H = MLA_HEADS
    full = lambda a: pl.BlockSpec(a.shape, lambda b, i: (0,) * a.ndim)
    tok = lambda w: pl.BlockSpec((None, tm, w), lambda b, i: (b, i, 0))
    head = pl.BlockSpec((None, H, tm, SLOT), lambda b, i: (b, 0, i, 0))
    tab = pl.BlockSpec((tm, SLOT), lambda b, i: (i, 0))
    ws = [p["mix_g"], p["w_in"], p["q_g"], p["w_uq"], p["kv_g"], p["w_uk"], p["w_uv"], p["qh_g"], p["kh_g"]]
    return pl.pallas_call(
        functools.partial(_inproj_kernel, tm=tm, n_pad=n_pad),
        grid=(B, nb),
        in_specs=[tok(D)] + [full(w) for w in ws] + [tab, tab, tab],
        out_specs=[tok(LRU_WIDTH), tok(LRU_WIDTH), head,
                   pl.BlockSpec((None, H, SLOT, tm), lambda b, i: (b, 0, 0, i)), head],
        out_shape=[jax.ShapeDtypeStruct((B, Tp, LRU_WIDTH), F32), jax.ShapeDtypeStruct((B, Tp, LRU_WIDTH), F32),
                   jax.ShapeDtypeStruct((B, H, Tp, SLOT), BF16), jax.ShapeDtypeStruct((B, H, SLOT, Tp), BF16),
                   jax.ShapeDtypeStruct((B, H, Tp, SLOT), BF16)],
        compiler_params=_params("parallel", "parallel"),
        name="inproj",
    )(h, *ws, *rope_tabs)


def _lru_kernel(xl_ref, gg_ref, cw_ref, cb_ref, wa_ref, ba_ref, wx_ref, bx_ref, lam_ref, og_ref, y_ref,
                ext_ref, hst_ref, *, tt, n_pad):
    t = pl.program_id(1)
    HALO = 8

    @pl.when(t == 0)
    def _():
        ext_ref[0:HALO, :] = jnp.zeros((HALO, LRU_WIDTH), F32)
        hst_ref[...] = jnp.zeros((HALO, LRU_WIDTH), F32)

    row = t * tt + lax.broadcasted_iota(jnp.int32, (tt, 1), 0)
    real = row >= n_pad
    ext_ref[HALO:HALO + tt, :] = jnp.where(real, xl_ref[...], 0.0)
    xc = cb_ref[...]
    for k in range(CONV_WIDTH):
        off = HALO - (CONV_WIDTH - 1) + k
        xc = xc + ext_ref[off:off + tt, :] * cw_ref[k:k + 1, :]
    xcb = xc.astype(BF16)
    r = jax.nn.sigmoid(_dot(xcb, wa_ref[...]) + ba_ref[...])
    i = jax.nn.sigmoid(_dot(xcb, wx_ref[...]) + bx_ref[...])
    nl = -lam_ref[...]
    softplus = jnp.maximum(nl, 0.0) + jnp.log1p(jnp.exp(-jnp.abs(nl)))
    log_a = -LRU_C * r * softplus
    a = jnp.exp(log_a)
    mult = jnp.sqrt(1.0 - jnp.exp(2.0 * log_a))
    b = jnp.where(real, mult * (i * xc), 0.0)
    ridx = lax.broadcasted_iota(jnp.int32, (tt, 1), 0)
    s = 1
    while s < tt:
        keep = ridx >= s
        a_sh = jnp.where(keep, pltpu.roll(a, s, 0), 1.0)
        b_sh = jnp.where(keep, pltpu.roll(b, s, 0), 0.0)
        b = a * b_sh + b
        a = a * a_sh
        s *= 2
    hprev = hst_ref[HALO - 1:HALO, :]
    hs = a * hprev + b
    hst_ref[...] = hs[tt - HALO:tt, :]
    ext_ref[0:HALO, :] = ext_ref[tt:tt + HALO, :]
    y = hs * gg_ref[...]
    y_ref[...] = _rms(y, og_ref[...]).astype(BF16)


def _lru(xl, gg, p, *, tt, n_pad):
    B, Tp, W = xl.shape
    full = lambda a: pl.BlockSpec(a.shape, lambda b, i: (0,) * a.ndim)
    tok = pl.BlockSpec((None, tt, W), lambda b, i: (b, i, 0))
    ws = [p["conv_w"], p["conv_b"], p["wa"], p["ba"], p["wx"], p["bx"], p["lam"], p["lru_og"]]
    return pl.pallas_call(
        functools.partial(_lru_kernel, tt=tt, n_pad=n_pad),
        grid=(B, Tp // tt),
        in_specs=[tok, tok] + [full(w) for w in ws],
        out_specs=tok,
        out_shape=jax.ShapeDtypeStruct((B, Tp, W), BF16),
        scratch_shapes=[pltpu.VMEM((tt + 8, W), F32), pltpu.VMEM((8, W), F32)],
        compiler_params=_params("parallel", "arbitrary"),
        name="rglru",
    )(xl, gg, *ws)


HEADS_PER_STEP = SLOT // V_DIM


def _attn_kernel(q_ref, k_ref, v_ref, o_ref, *, tq, tk):
    qi = pl.program_id(2)
    kd = (qi * tq) // tk

    def step(kb, carry, diagonal):
        start = pl.multiple_of(kb * tk, tk)
        out = []
        for hh in range(HEADS_PER_STEP):
            m, acc = carry[hh]
            s = _dot(q_ref[hh], k_ref[hh, :, pl.ds(start, tk)])
            if diagonal:
                qpos = qi * tq + lax.broadcasted_iota(jnp.int32, (tq, tk), 0)
                kpos = kb * tk + lax.broadcasted_iota(jnp.int32, (tq, tk), 1)
                s = jnp.where(kpos <= qpos, s, NEG_INF)
            m_new = jnp.maximum(m, jnp.max(s, axis=-1, keepdims=True))
            alpha = jnp.exp(m - m_new)
            pr = jnp.exp(s - m_new).astype(BF16)
            out.append((m_new, acc * alpha + _dot(pr, v_ref[hh, pl.ds(start, tk), :])))
        return tuple(out)

    init = tuple((jnp.full((tq, 1), M_INIT, F32), jnp.zeros((tq, SLOT), F32)) for _ in range(HEADS_PER_STEP))
    carry = lax.fori_loop(0, kd, lambda kb, c: step(kb, c, False), init)
    carry = step(kd, carry, True)
    lane = lax.broadcasted_iota(jnp.int32, (1, SLOT), 1)
    o = jnp.zeros((tq, SLOT), F32)
    for hh in range(HEADS_PER_STEP):
        acc = carry[hh][1]
        oh = acc / acc[:, ONES_LANE:ONES_LANE + 1]
        if hh:
            oh = pltpu.roll(oh, hh * V_DIM, 1)
        o = jnp.where((lane >= hh * V_DIM) & (lane < (hh + 1) * V_DIM), oh, o)
    o_ref[...] = o.astype(BF16)


def _attention(q, k, v, *, tq, tk):
    B, H, Tp, _ = q.shape
    G = HEADS_PER_STEP
    qspec = pl.BlockSpec((None, G, tq, SLOT), lambda b, h, i: (b, h, i, 0))
    kvspec = pl.BlockSpec((None, G, Tp, SLOT), lambda b, h, i: (b, h, 0, 0))
    return pl.pallas_call(
        functools.partial(_attn_kernel, tq=tq, tk=tk),
        grid=(B, H // G, Tp // tq),
        in_specs=[qspec, pl.BlockSpec((None, G, SLOT, Tp), lambda b, h, i: (b, h, 0, 0)), kvspec],
        out_specs=pl.BlockSpec((None, tq, SLOT), lambda b, h, i: (b, i, h)),
        out_shape=jax.ShapeDtypeStruct((B, Tp, H * V_DIM), BF16),
        compiler_params=_params("parallel", "parallel", "arbitrary"),
        name="attention",
    )(q, k, v)


def _extract_top(x, n, payload=None):
    R, L = x.shape
    pos = lax.broadcasted_iota(jnp.int32, (R, L), 0)
    out_row = lax.broadcasted_iota(jnp.int32, (n, L), 0)
    vals = jnp.zeros((n, L), F32)
    idxs = jnp.zeros((n, L), jnp.int32)
    for it in range(n):
        m = jnp.max(x, axis=0, keepdims=True)
        idx = jnp.min(jnp.where(x == m, pos, R), axis=0, keepdims=True)
        hit = pos == idx
        if payload is None:
            sel = idx
        else:
            sel = jnp.sum(jnp.where(hit, payload, 0), axis=0, keepdims=True)
        vals = jnp.where(out_row == it, m, vals)
        idxs = jnp.where(out_row == it, sel, idxs)
        x = jnp.where(hit, -jnp.inf, x)
    return vals, idxs


def _candidates(ts0, ti0, ts1, ti1):
    K = PEER_TOPK
    sub = lax.broadcasted_iota(jnp.int32, (8, ts0.shape[1]), 0)
    cs = [ts0[0:1, :] + ts1]
    ci = [ti0[0:1, :] * N_KEYS + ti1]
    for a in range(1, 8):
        nb = K // (a + 1)
        piece = ts0[a:a + 1, :] + ts1[0:8, :]
        cs.append(piece if nb >= 8 else jnp.where(sub < nb, piece, -jnp.inf))
        ci.append(ti0[a:a + 1, :] * N_KEYS + ti1[0:8, :])
    cs.append(ts0[8:K, :] + ts1[0:1, :])
    ci.append(ti0[8:K, :] * N_KEYS + ti1[0:1, :])
    return jnp.concatenate(cs, axis=0), jnp.concatenate(ci, axis=0)


def _outproj_kernel(h_ref, yl_ref, o_ref, ag_ref, wol_ref, woa_ref, fg_ref, wq_ref, sk_ref,
                    h2_ref, hn_ref, et_ref, gt_ref, qp_ref, s_ref, *, tm):
    an = _rms(o_ref[...].astype(F32), ag_ref[...]).astype(BF16)
    h2 = h_ref[...] + _dot(yl_ref[...], wol_ref[...]) + _dot(an, woa_ref[...])
    h2_ref[...] = h2
    hn = _rms(h2, fg_ref[...]).astype(BF16)
    hn_ref[...] = hn
    qp_ref[...] = _dot(hn, wq_ref[...])
    K = PEER_TOPK
    nchunk = tm // LANE

    def head_body(hd, carry):
        for pp in range(2):
            off = pl.multiple_of((hd * 2 + pp) * D_SUBKEY, D_SUBKEY)
            qs = qp_ref[:, pl.ds(off, D_SUBKEY)].astype(BF16)
            s_ref[pp] = _dot_nt(sk_ref[pp], qs)

        def chunk_body(j, c2):
            lo = pl.multiple_of(j * LANE, LANE)
            ts0, ti0 = _extract_top(s_ref[0, :, pl.ds(lo, LANE)], K)
            ts1, ti1 = _extract_top(s_ref[1, :, pl.ds(lo, LANE)], K)
            cs, ci = _candidates(ts0, ti0, ts1, ti1)
            bs, be = _extract_top(cs, K, payload=ci)
            e = jnp.exp(bs - bs[0:1, :])
            g = e / jnp.sum(e, axis=0, keepdims=True)
            r0 = pl.multiple_of(hd * K, K)
            et_ref[pl.ds(r0, K), pl.ds(lo, LANE)] = be
            gt_ref[pl.ds(r0, K), pl.ds(lo, LANE)] = g
            return c2

        lax.fori_loop(0, nchunk, chunk_body, 0)
        return carry

    lax.fori_loop(0, PEER_HEADS, head_body, 0)


def _outproj(h, yl, o, p, *, tm):
    B, Tp, D = h.shape
    nb = Tp // tm
    H = MLA_HEADS
    HK = PEER_HEADS * PEER_TOPK
    full = lambda a: pl.BlockSpec(a.shape, lambda b, i: (0,) * a.ndim)
    tok = lambda w: pl.BlockSpec((None, tm, w), lambda b, i: (b, i, 0))
    sel = pl.BlockSpec((HK, tm), lambda b, i: (0, b * nb + i))
    ws = [p["att_g"], p["w_out_lru"], p["w_out_att"], p["ffn_g"], p["peer_wq"], p["subkeys"]]
    return pl.pallas_call(
        functools.partial(_outproj_kernel, tm=tm),
        grid=(B, nb),
        in_specs=[tok(D), tok(LRU_WIDTH), tok(H * V_DIM)] + [full(w) for w in ws],
        out_specs=[tok(D), tok(D), sel, sel],
        out_shape=[jax.ShapeDtypeStruct((B, Tp, D), F32), jax.ShapeDtypeStruct((B, Tp, D), BF16),
                   jax.ShapeDtypeStruct((HK, B * Tp), jnp.int32), jax.ShapeDtypeStruct((HK, B * Tp), F32)],
        scratch_shapes=[pltpu.VMEM((tm, PEER_HEADS * 2 * D_SUBKEY), F32), pltpu.VMEM((2, N_KEYS, tm), F32)],
        compiler_params=_params("parallel", "parallel"),
        name="outproj_topk",
    )(h, yl, o, *ws)


def _build_slot_tables(c, et_ref, gt_ref, tabi_ref, tabg_ref, eperm_ref, gov_ref, rcnt_ref, *, rcap, rs):
    HK = PEER_HEADS * PEER_TOPK
    NV = HK // 8
    lo = pl.multiple_of(c * LANE, LANE)
    ex = et_ref[:, pl.ds(lo, LANE)]
    gx = gt_ref[:, pl.ds(lo, LANE)]
    ik = ex >> 7
    dk = (ik + (ex & (N_KEYS - 1))) & (N_KEYS - 1)
    dv = [dk[v * 8:(v + 1) * 8, :] for v in range(NV)]
    rank = [jnp.zeros((8, LANE), jnp.int32) for _ in range(NV)]
    sub = lax.broadcasted_iota(jnp.int32, (8, LANE), 0)
    for kp in range(HK - 1):
        v0, r0 = divmod(kp, 8)
        row = dv[v0][r0:r0 + 1, :]
        if r0 < 7:
            rank[v0] = rank[v0] + jnp.where(sub > r0, jnp.where(dv[v0] == row, 1, 0), 0)
        for v in range(v0 + 1, NV):
            rank[v] = rank[v] + jnp.where(dv[v] == row, 1, 0)
    rank = jnp.concatenate(rank, axis=0)
    key = rank * N_KEYS + dk
    rcnt_ref[c] = jnp.max(rank) + 1
    eperm_ref[:, pl.ds(lo, LANE)] = ik * N_KEYS + dk
    gov_ref[:, pl.ds(lo, LANE)] = jnp.where(rank >= rcap, gx, 0.0)
    dio = lax.broadcasted_iota(jnp.int32, (N_KEYS, LANE), 0)

    def round_body(r, carry):
        want = dio + r * N_KEYS
        ti = jnp.full((N_KEYS, LANE), -1, jnp.int32)
        tg = jnp.zeros((N_KEYS, LANE), F32)
        for k in range(HK):
            hit = key[k:k + 1, :] == want
            ti = jnp.where(hit, ik[k:k + 1, :], ti)
            tg = jnp.where(hit, gx[k:k + 1, :], tg)
        r0 = pl.multiple_of(r * N_KEYS, N_KEYS)
        tabi_ref[pl.ds(r0, N_KEYS), pl.ds(lo, LANE)] = ti
        tabg_ref[pl.ds(r0, N_KEYS), pl.ds(lo, LANE)] = tg
        return carry

    nbuilt = jnp.minimum(rcnt_ref[c], rcap)
    lax.fori_loop(0, nbuilt, round_body, 0)

    def empty_body(r, carry):
        r0 = pl.multiple_of(r * N_KEYS, N_KEYS)
        tabi_ref[pl.ds(r0, N_KEYS), pl.ds(lo, LANE)] = jnp.full((N_KEYS, LANE), -1, jnp.int32)
        tabg_ref[pl.ds(r0, N_KEYS), pl.ds(lo, LANE)] = jnp.zeros((N_KEYS, LANE), F32)
        return carry

    lax.fori_loop(nbuilt, rs, empty_body, 0)


def _peer_kernel(hn_ref, et_ref, gt_ref, u_ref, vt_ref, h2_ref, out_ref, pre_ref, act_ref, gx_ref, acc_ref,
                 tabi_ref, tabg_ref, eperm_ref, gov_ref, rcnt_ref, *, tm, te, rcap, rs):
    e = pl.program_id(1)
    HK = PEER_HEADS * PEER_TOPK
    nchunk = tm // LANE
    ntile = te // N_KEYS
    i_base = (e - 1) * ntile

    @pl.when(e == 0)
    def _():
        acc_ref[...] = jnp.zeros(acc_ref.shape, F32)
        pre_ref[...] = jnp.zeros(pre_ref.shape, F32)
        gx_ref[...] = jnp.zeros(gx_ref.shape, F32)

        def build(c, carry):
            _build_slot_tables(c, et_ref, gt_ref, tabi_ref, tabg_ref, eperm_ref, gov_ref, rcnt_ref,
                               rcap=rcap, rs=rs)
            return carry

        lax.fori_loop(0, nchunk, build, 0)

    def deep_chunk(c, carry):
        nround = rcnt_ref[c]

        @pl.when(nround > rs)
        def _():
            lo = pl.multiple_of(c * LANE, LANE)
            ntab = jnp.minimum(nround, rcap)
            nover = jnp.where(nround > rcap, HK // 8, 0)

            def tile_body(it, c2):
                i_abs = i_base + it
                eio = i_abs * N_KEYS + lax.broadcasted_iota(jnp.int32, (N_KEYS, LANE), 0)

                def round_body(r, g):
                    t0 = pl.multiple_of(r * N_KEYS, N_KEYS)
                    ti = tabi_ref[pl.ds(t0, N_KEYS), pl.ds(lo, LANE)]
                    tg = tabg_ref[pl.ds(t0, N_KEYS), pl.ds(lo, LANE)]
                    return g + jnp.where(ti == i_abs, tg, 0.0)

                def over_body(k8, g):
                    k0 = pl.multiple_of(k8 * 8, 8)
                    eb = eperm_ref[pl.ds(k0, 8), pl.ds(lo, LANE)]
                    gb = gov_ref[pl.ds(k0, 8), pl.ds(lo, LANE)]
                    for r in range(8):
                        g = g + jnp.where(eb[r:r + 1, :] == eio, gb[r:r + 1, :], 0.0)
                    return g

                g = lax.fori_loop(rs, ntab, round_body, jnp.zeros((N_KEYS, LANE), F32))
                g = lax.fori_loop(0, nover, over_body, g)
                gx_ref[pl.ds(pl.multiple_of(it * N_KEYS, N_KEYS), N_KEYS), pl.ds(lo, LANE)] = g
                return c2

            lax.fori_loop(0, ntile, tile_body, 0)

        return carry

    lax.fori_loop(0, nchunk, deep_chunk, 0)

    for it in range(ntile):
        rows = slice(it * N_KEYS, (it + 1) * N_KEYS)
        for c in range(nchunk):
            lanes = slice(c * LANE, (c + 1) * LANE)
            g = gx_ref[rows, lanes]
            for r in range(rs):
                trow = slice(r * N_KEYS, (r + 1) * N_KEYS)
                g = g + jnp.where(tabi_ref[trow, lanes] == i_base + it, tabg_ref[trow, lanes], 0.0)
            act_ref[rows, lanes] = (_gelu(pre_ref[rows, lanes]) * g).astype(BF16)
    acc_ref[...] += _dot(vt_ref[...], act_ref[...])
    pre_ref[...] = _dot_nt(u_ref[...], hn_ref[...])

    @pl.when(e == pl.num_programs(1) - 1)
    def _():
        out_ref[...] = h2_ref[...] + acc_ref[...].T


def _peer(h2, hn, et, gt, u, vt, *, tm, te, rcap=16, rs=6):
    N, D = h2.shape
    E = u.shape[0]
    HK = et.shape[0]
    ne = E // te
    assert rs <= rcap
    tok = pl.BlockSpec((tm, D), lambda i, e: (i, 0))
    sel = pl.BlockSpec((HK, tm), lambda i, e: (0, i))
    return pl.pallas_call(
        functools.partial(_peer_kernel, tm=tm, te=te, rcap=rcap, rs=rs),
        grid=(N // tm, ne + 1),
        in_specs=[tok, sel, sel, pl.BlockSpec((te, D), lambda i, e: (jnp.minimum(e, ne - 1), 0)),
                  pl.BlockSpec((D, te), lambda i, e: (0, jnp.maximum(e - 1, 0))), tok],
        out_specs=tok,
        out_shape=jax.ShapeDtypeStruct((N, D), F32),
        scratch_shapes=[pltpu.VMEM((te, tm), F32), pltpu.VMEM((te, tm), BF16), pltpu.VMEM((te, tm), F32),
                        pltpu.VMEM((D, tm), F32),
                        pltpu.VMEM((rcap * N_KEYS, tm), jnp.int32), pltpu.VMEM((rcap * N_KEYS, tm), F32),
                        pltpu.VMEM((HK, tm), jnp.int32), pltpu.VMEM((HK, tm), F32),
                        pltpu.SMEM((tm // LANE,), jnp.int32)],
        compiler_params=_params("parallel", "arbitrary"),
        name="peer_experts",
    )(hn, et, gt, u, vt, h2)


def _diag_experts(w):
    D = w.shape[-1]
    i = jnp.arange(N_KEYS)[:, None]
    d = jnp.arange(N_KEYS)[None, :]
    j = (d - i) % N_KEYS
    return jnp.take_along_axis(w.reshape(N_KEYS, N_KEYS, D), j[:, :, None], axis=1).reshape(N_KEYS * N_KEYS, D)


def _slots(w, width):
    lead = w.shape[:-1]
    w = w.reshape(*lead, -1, width)
    w = jnp.pad(w, [(0, 0)] * len(lead) + [(0, 0), (0, SLOT - width)])
    return w.reshape(*lead, -1)


def _block_diag(w):
    n, hd, _ = w.shape
    eye = jnp.eye(n, dtype=w.dtype)
    return (eye[:, None, :, None] * w[:, :, None, :]).reshape(n * hd, n * hd)


def _rope_tables(Tp, n_pad):
    half = QK_ROPE // 2
    pos = (jnp.arange(Tp) - n_pad).astype(F32)
    freq = ROPE_THETA ** (-jnp.arange(half, dtype=F32) / half)
    ang = pos[:, None] * freq[None, :]
    cos, sin = jnp.cos(ang), jnp.sin(ang)
    z = lambda n: jnp.zeros((Tp, n), F32)
    tail = SLOT - QK_DIM
    rc = jnp.concatenate([jnp.ones((Tp, QK_NOPE), F32), cos, cos, jnp.ones((Tp, tail), F32)], axis=1)
    rs1 = jnp.concatenate([z(QK_NOPE), -sin, z(half), z(tail)], axis=1)
    rs2 = jnp.concatenate([z(QK_NOPE), z(half), sin, z(tail)], axis=1)
    return rc, rs1, rs2


def _layer_params(l, mix_norm_g, w_in, conv_w, conv_b, lru_wa, lru_ba, lru_wx, lru_bx, lru_lambda,
                  q_lora_norm_g, w_uq, kv_lora_norm_g, w_ukv, q_head_norm_g, k_head_norm_g, lru_out_norm_g,
                  attn_out_norm_g, w_out, ffn_norm_g, peer_wq, peer_subkeys, peer_u, peer_v):
    H = MLA_HEADS
    row = lambda a: a[None, :].astype(F32)
    wi = w_in[l]
    split4 = 2 * LRU_WIDTH + Q_LORA + KV_LORA
    kr_cols = jnp.pad(wi[:, split4:], ((0, 0), (QK_NOPE, SLOT - QK_DIM)))
    wkv = w_ukv[l].reshape(KV_LORA, H, QK_NOPE + V_DIM)
    return {
        "mix_g": row(mix_norm_g[l]),
        "w_in": jnp.concatenate([wi[:, :split4], kr_cols], axis=1).astype(BF16),
        "q_g": row(q_lora_norm_g[l]),
        "w_uq": _slots(w_uq[l], QK_DIM).astype(BF16),
        "kv_g": row(kv_lora_norm_g[l]),
        "w_uk": _slots(wkv[:, :, :QK_NOPE].reshape(KV_LORA, H * QK_NOPE), QK_NOPE).astype(BF16),
        "w_uv": _slots(wkv[:, :, QK_NOPE:].reshape(KV_LORA, H * V_DIM), V_DIM).astype(BF16),
        "qh_g": _slots(row(q_head_norm_g[l]), QK_DIM),
        "kh_g": _slots(row(k_head_norm_g[l]), QK_DIM),
        "conv_w": conv_w[l],
        "conv_b": row(conv_b[l]),
        "wa": _block_diag(lru_wa[l]).astype(BF16),
        "ba": row(lru_ba[l].reshape(-1)),
        "wx": _block_diag(lru_wx[l]).astype(BF16),
        "bx": row(lru_bx[l].reshape(-1)),
        "lam": row(lru_lambda[l]),
        "lru_og": row(lru_out_norm_g[l]),
        "att_g": row(attn_out_norm_g[l]),
        "w_out_lru": w_out[l][:LRU_WIDTH].astype(BF16),
        "w_out_att": w_out[l][LRU_WIDTH:].astype(BF16),
        "ffn_g": row(ffn_norm_g[l]),
        "peer_wq": peer_wq[l].astype(BF16),
        "subkeys": peer_subkeys[l].astype(BF16),
        "u": _diag_experts(peer_u[l]).astype(BF16),
        "vt": _diag_experts(peer_v[l]).T.astype(BF16),
    }


def _tiles(B, Tp):
    nblk = Tp // ATTN_BLOCK
    per_tok = ATTN_BLOCK * max(d for d in range(1, 6) if nblk % d == 0)
    n = B * nblk
    flat = ATTN_BLOCK * next(d for d in (4, 2, 5, 3, 1) if n % d == 0)
    return per_tok, flat


def kernel(x, meta_tokens, mix_norm_g, w_in, conv_w, conv_b, lru_wa, lru_ba, lru_wx, lru_bx, lru_lambda,
           q_lora_norm_g, w_uq, kv_lora_norm_g, w_ukv, q_head_norm_g, k_head_norm_g, lru_out_norm_g,
           attn_out_norm_g, w_out, ffn_norm_g, peer_wq, peer_subkeys, peer_u, peer_v):
    B, S, D = x.shape
    depth = w_in.shape[0]
    T = N_META + S
    n_pad = (-T) % ATTN_BLOCK
    Tp = T + n_pad
    tm, tflat = _tiles(B, Tp)
    meta = jnp.broadcast_to(meta_tokens.astype(x.dtype)[None], (B, N_META, D))
    h = jnp.concatenate([jnp.zeros((B, n_pad, D), x.dtype), meta, x], axis=1)
    rope_tabs = _rope_tables(Tp, n_pad)
    weights = (mix_norm_g, w_in, conv_w, conv_b, lru_wa, lru_ba, lru_wx, lru_bx, lru_lambda, q_lora_norm_g,
               w_uq, kv_lora_norm_g, w_ukv, q_head_norm_g, k_head_norm_g, lru_out_norm_g, attn_out_norm_g,
               w_out, ffn_norm_g, peer_wq, peer_subkeys, peer_u, peer_v)
    for l in range(depth):
        p = _layer_params(l, *weights)
        xl, gg, q, k, v = _inproj(h, p, rope_tabs, tm=tm, n_pad=n_pad)
        yl = _lru(xl, gg, p, tt=tm, n_pad=n_pad)
        o = _attention(q, k, v, tq=tm // 2 if (tm // 2) % 8 == 0 else tm, tk=tm)
        h2, hn, et, gt = _outproj(h, yl, o, p, tm=tm)
        h = _peer(h2.reshape(B * Tp, D), hn.reshape(B * Tp, D), et, gt, p["u"], p["vt"],
                  tm=tflat, te=1024).reshape(B, Tp, D)
    return h[:, n_pad:][:, N_META:]
```

```python
import functools
import math

import jax
import jax.numpy as jnp
from jax import lax
from jax.experimental import pallas as pl
from jax.experimental.pallas import tpu as pltpu

N_META = 16
LRU_WIDTH = 512
LRU_HEADS = 8
LRU_HEAD_DIM = LRU_WIDTH // LRU_HEADS
CONV_WIDTH = 4
LRU_C = 8.0
MLA_HEADS = 8
Q_LORA = 384
KV_LORA = 256
QK_NOPE = 64
QK_ROPE = 32
QK_DIM = QK_NOPE + QK_ROPE
V_DIM = 64
ROPE_THETA = 10000.0
ATTN_BLOCK = 128
PEER_HEADS = 8
N_KEYS = 128
PEER_TOPK = 16
D_SUBKEY = 128
EPS = 1e-6
NEG_INF = -1e30

LANE = 128
SLOT = LANE
MASK_LANE = QK_DIM
ONES_LANE = V_DIM
M_INIT = -3.0e38
VMEM_LIMIT = 56 * 1024 * 1024

F32 = jnp.float32
BF16 = jnp.bfloat16
INV_SQRT2 = 1.0 / math.sqrt(2.0)


def _gelu(x):
    return 0.5 * x * (1.0 + lax.erf(x * INV_SQRT2))


def _rms(x, g, n=None):
    n = x.shape[-1] if n is None else n
    ms = jnp.sum(x * x, axis=-1, keepdims=True) * (1.0 / n)
    return x * lax.rsqrt(ms + EPS) * g


def _dot(a, b):
    return jnp.dot(a, b, preferred_element_type=F32)


def _dot_nt(a, b):
    return lax.dot_general(a, b, (((1,), (1,)), ((), ())), preferred_element_type=F32)


def _params(*sem):
    return pltpu.CompilerParams(dimension_semantics=sem, vmem_limit_bytes=VMEM_LIMIT)


def _inproj_kernel(h_ref, g_ref, win_ref, qg_ref, wuq_ref, kvg_ref, wuk_ref, wuv_ref, qhg_ref, khg_ref,
                   rc_ref, rs1_ref, rs2_ref, xl_ref, gg_ref, q_ref, k_ref, v_ref, *, tm, n_pad):
    h = h_ref[...]
    hn = _rms(h, g_ref[...])
    z = _dot(hn.astype(BF16), win_ref[...])
    o1 = LRU_WIDTH
    o2 = 2 * LRU_WIDTH
    o3 = o2 + Q_LORA
    o4 = o3 + KV_LORA
    xl_ref[...] = z[:, :o1]
    gg_ref[...] = _gelu(z[:, o1:o2])
    cqn = _rms(z[:, o2:o3], qg_ref[...]).astype(BF16)
    ckvn = _rms(z[:, o3:o4], kvg_ref[...]).astype(BF16)
    kr = z[:, o4:o4 + SLOT]
    q = _dot(cqn, wuq_ref[...])
    kn = _dot(ckvn, wuk_ref[...])
    v = _dot(ckvn, wuv_ref[...])
    rc = rc_ref[...]
    rs1 = rs1_ref[...]
    rs2 = rs2_ref[...]
    half = QK_ROPE // 2
    lane = lax.broadcasted_iota(jnp.int32, (1, SLOT), 1)
    mask_lane = (lane == MASK_LANE).astype(F32)
    ones_lane = (lane == ONES_LANE).astype(F32)
    row = pl.program_id(1) * tm + lax.broadcasted_iota(jnp.int32, (tm, 1), 0)
    kbias = jnp.where(row < n_pad, NEG_INF, 0.0) * mask_lane
    scale = 1.0 / math.sqrt(QK_DIM)

    def rope(x):
        return x * rc + pltpu.roll(x, SLOT - half, 1) * rs1 + pltpu.roll(x, half, 1) * rs2

    for hd in range(MLA_HEADS):
        sl = slice(hd * SLOT, (hd + 1) * SLOT)
        qs = rope(_rms(q[:, sl], qhg_ref[...], QK_DIM))
        q_ref[hd] = (qs * scale + mask_lane).astype(BF16)
        ks = rope(_rms(kn[:, sl] + kr, khg_ref[...], QK_DIM))
        k_ref[hd] = (ks + kbias).T.astype(BF16)
        v_ref[hd] = (v[:, sl] + ones_lane).astype(BF16)


def _inproj(h, p, rope_tabs, *, tm, n_pad):
    B, Tp, D = h.shape
    nb = Tp // tm
    H = MLA_HEADS
    full = lambda a: pl.BlockSpec(a.shape, lambda b, i: (0,) * a.ndim)
    tok = lambda w: pl.BlockSpec((None, tm, w), lambda b, i: (b, i, 0))
    head = pl.BlockSpec((None, H, tm, SLOT), lambda b, i: (b, 0, i, 0))
    tab = pl.BlockSpec((tm, SLOT), lambda b, i: (i, 0))
    ws = [p["mix_g"], p["w_in"], p["q_g"], p["w_uq"], p["kv_g"], p["w_uk"], p["w_uv"], p["qh_g"], p["kh_g"]]
    return pl.pallas_call(
        functools.partial(_inproj_kernel, tm=tm, n_pad=n_pad),
        grid=(B, nb),
        in_specs=[tok(D)] + [full(w) for w in ws] + [tab, tab, tab],
        out_specs=[tok(LRU_WIDTH), tok(LRU_WIDTH), head,
                   pl.BlockSpec((None, H, SLOT, tm), lambda b, i: (b, 0, 0, i)), head],
        out_shape=[jax.ShapeDtypeStruct((B, Tp, LRU_WIDTH), F32), jax.ShapeDtypeStruct((B, Tp, LRU_WIDTH), F32),
                   jax.ShapeDtypeStruct((B, H, Tp, SLOT), BF16), jax.ShapeDtypeStruct((B, H, SLOT, Tp), BF16),
                   jax.ShapeDtypeStruct((B, H, Tp, SLOT), BF16)],
        compiler_params=_params("parallel", "parallel"),
        name="inproj",
    )(h, *ws, *rope_tabs)


def _lru_kernel(xl_ref, gg_ref, cw_ref, cb_ref, wa_ref, ba_ref, wx_ref, bx_ref, lam_ref, og_ref, y_ref,
                ext_ref, hst_ref, *, tt, n_pad):
    t = pl.program_id(1)
    HALO = 8

    @pl.when(t == 0)
    def _():
        ext_ref[0:HALO, :] = jnp.zeros((HALO, LRU_WIDTH), F32)
        hst_ref[...] = jnp.zeros((HALO, LRU_WIDTH), F32)

    row = t * tt + lax.broadcasted_iota(jnp.int32, (tt, 1), 0)
    real = row >= n_pad
    ext_ref[HALO:HALO + tt, :] = jnp.where(real, xl_ref[...], 0.0)
    xc = cb_ref[...]
    for k in range(CONV_WIDTH):
        off = HALO - (CONV_WIDTH - 1) + k
        xc = xc + ext_ref[off:off + tt, :] * cw_ref[k:k + 1, :]
    xcb = xc.astype(BF16)
    r = jax.nn.sigmoid(_dot(xcb, wa_ref[...]) + ba_ref[...])
    i = jax.nn.sigmoid(_dot(xcb, wx_ref[...]) + bx_ref[...])
    nl = -lam_ref[...]
    softplus = jnp.maximum(nl, 0.0) + jnp.log1p(jnp.exp(-jnp.abs(nl)))
    log_a = -LRU_C * r * softplus
    a = jnp.exp(log_a)
    mult = jnp.sqrt(1.0 - jnp.exp(2.0 * log_a))
    b = jnp.where(real, mult * (i * xc), 0.0)
    ridx = lax.broadcasted_iota(jnp.int32, (tt, 1), 0)
    s = 1
    while s < tt:
        keep = ridx >= s
        a_sh = jnp.where(keep, pltpu.roll(a, s, 0), 1.0)
        b_sh = jnp.where(keep, pltpu.roll(b, s, 0), 0.0)
        b = a * b_sh + b
        a = a * a_sh
        s *= 2
    hprev = hst_ref[HALO - 1:HALO, :]
    hs = a * hprev + b
    hst_ref[...] = hs[tt - HALO:tt, :]
    ext_ref[0:HALO, :] = ext_ref[tt:tt + HALO, :]
    y = hs * gg_ref[...]
    y_ref[...] = _rms(y, og_ref[...]).astype(BF16)


def _lru(xl, gg, p, *, tt, n_pad):
    B, Tp, W = xl.shape
    full = lambda a: pl.BlockSpec(a.shape, lambda b, i: (0,) * a.ndim)
    tok = pl.BlockSpec((None, tt, W), lambda b, i: (b, i, 0))
    ws = [p["conv_w"], p["conv_b"], p["wa"], p["ba"], p["wx"], p["bx"], p["lam"], p["lru_og"]]
    return pl.pallas_call(
        functools.partial(_lru_kernel, tt=tt, n_pad=n_pad),
        grid=(B, Tp // tt),
        in_specs=[tok, tok] + [full(w) for w in ws],
        out_specs=tok,
        out_shape=jax.ShapeDtypeStruct((B, Tp, W), BF16),
        scratch_shapes=[pltpu.VMEM((tt + 8, W), F32), pltpu.VMEM((8, W), F32)],
        compiler_params=_params("parallel", "arbitrary"),
        name="rglru",
    )(xl, gg, *ws)


HEADS_PER_STEP = SLOT // V_DIM


def _attn_kernel(q_ref, k_ref, v_ref, o_ref, *, tq, tk):
    qi = pl.program_id(2)
    kd = (qi * tq) // tk

    def step(kb, carry, diagonal):
        start = pl.multiple_of(kb * tk, tk)
        out = []
        for hh in range(HEADS_PER_STEP):
            m, acc = carry[hh]
            s = _dot(q_ref[hh], k_ref[hh, :, pl.ds(start, tk)])
            if diagonal:
                qpos = qi * tq + lax.broadcasted_iota(jnp.int32, (tq, tk), 0)
                kpos = kb * tk + lax.broadcasted_iota(jnp.int32, (tq, tk), 1)
                s = jnp.where(kpos <= qpos, s, NEG_INF)
            m_new = jnp.maximum(m, jnp.max(s, axis=-1, keepdims=True))
            alpha = jnp.exp(m - m_new)
            pr = jnp.exp(s - m_new).astype(BF16)
            out.append((m_new, acc * alpha + _dot(pr, v_ref[hh, pl.ds(start, tk), :])))
        return tuple(out)

    init = tuple((jnp.full((tq, 1), M_INIT, F32), jnp.zeros((tq, SLOT), F32)) for _ in range(HEADS_PER_STEP))
    carry = lax.fori_loop(0, kd, lambda kb, c: step(kb, c, False), init)
    carry = step(kd, carry, True)
    lane = lax.broadcasted_iota(jnp.int32, (1, SLOT), 1)
    o = jnp.zeros((tq, SLOT), F32)
    for hh in range(HEADS_PER_STEP):
        acc = carry[hh][1]
        oh = acc / acc[:, ONES_LANE:ONES_LANE + 1]
        if hh:
            oh = pltpu.roll(oh, hh * V_DIM, 1)
        o = jnp.where((lane >= hh * V_DIM) & (lane < (hh + 1) * V_DIM), oh, o)
    o_ref[...] = o.astype(BF16)


def _attention(q, k, v, *, tq, tk):
    B, H, Tp, _ = q.shape
    G = HEADS_PER_STEP
    qspec = pl.BlockSpec((None, G, tq, SLOT), lambda b, h, i: (b, h, i, 0))
    kvspec = pl.BlockSpec((None, G, Tp, SLOT), lambda b, h, i: (b, h, 0, 0))
    return pl.pallas_call(
        functools.partial(_attn_kernel, tq=tq, tk=tk),
        grid=(B, H // G, Tp // tq),
        in_specs=[qspec, pl.BlockSpec((None, G, SLOT, Tp), lambda b, h, i: (b, h, 0, 0)), kvspec],
        out_specs=pl.BlockSpec((None, tq, SLOT), lambda b, h, i: (b, i, h)),
        out_shape=jax.ShapeDtypeStruct((B, Tp, H * V_DIM), BF16),
        compiler_params=_params("parallel", "parallel", "arbitrary"),
        name="attention",
    )(q, k, v)


def _extract_top(x, n, payload=None):
    R, L = x.shape
    pos = lax.broadcasted_iota(jnp.int32, (R, L), 0)
    out_row = lax.broadcasted_iota(jnp.int32, (n, L), 0)
    vals = jnp.zeros((n, L), F32)
    idxs = jnp.zeros((n, L), jnp.int32)
    for it in range(n):
        m = jnp.max(x, axis=0, keepdims=True)
        idx = jnp.min(jnp.where(x == m, pos, R), axis=0, keepdims=True)
        hit = pos == idx
        if payload is None:
            sel = idx
        else:
            sel = jnp.sum(jnp.where(hit, payload, 0), axis=0, keepdims=True)
        vals = jnp.where(out_row == it, m, vals)
        idxs = jnp.where(out_row == it, sel, idxs)
        x = jnp.where(hit, -jnp.inf, x)
    return vals, idxs


def _candidates(ts0, ti0, ts1, ti1):
    K = PEER_TOPK
    sub = lax.broadcasted_iota(jnp.int32, (8, ts0.shape[1]), 0)
    cs = [ts0[0:1, :] + ts1]
    ci = [ti0[0:1, :] * N_KEYS + ti1]
    for a in range(1, 8):
        nb = K // (a + 1)
        piece = ts0[a:a + 1, :] + ts1[0:8, :]
        cs.append(piece if nb >= 8 else jnp.where(sub < nb, piece, -jnp.inf))
        ci.append(ti0[a:a + 1, :] * N_KEYS + ti1[0:8, :])
    cs.append(ts0[8:K, :] + ts1[0:1, :])
    ci.append(ti0[8:K, :] * N_KEYS + ti1[0:1, :])
    return jnp.concatenate(cs, axis=0), jnp.concatenate(ci, axis=0)


def _outproj_kernel(h_ref, yl_ref, o_ref, ag_ref, wol_ref, woa_ref, fg_ref, wq_ref, sk_ref,
                    h2_ref, hn_ref, et_ref, gt_ref, qp_ref, s_ref, *, tm):
    an = _rms(o_ref[...].astype(F32), ag_ref[...]).astype(BF16)
    h2 = h_ref[...] + _dot(yl_ref[...], wol_ref[...]) + _dot(an, woa_ref[...])
    h2_ref[...] = h2
    hn = _rms(h2, fg_ref[...]).astype(BF16)
    hn_ref[...] = hn
    qp_ref[...] = _dot(hn, wq_ref[...])
    K = PEER_TOPK
    nchunk = tm // LANE

    def head_body(hd, carry):
        for pp in range(2):
            off = pl.multiple_of((hd * 2 + pp) * D_SUBKEY, D_SUBKEY)
            qs = qp_ref[:, pl.ds(off, D_SUBKEY)].astype(BF16)
            s_ref[pp] = _dot_nt(sk_ref[pp], qs)

        def chunk_body(j, c2):
            lo = pl.multiple_of(j * LANE, LANE)
            ts0, ti0 = _extract_top(s_ref[0, :, pl.ds(lo, LANE)], K)
            ts1, ti1 = _extract_top(s_ref[1, :, pl.ds(lo, LANE)], K)
            cs, ci = _candidates(ts0, ti0, ts1, ti1)
            bs, be = _extract_top(cs, K, payload=ci)
            e = jnp.exp(bs - bs[0:1, :])
            g = e / jnp.sum(e, axis=0, keepdims=True)
            r0 = pl.multiple_of(hd * K, K)
            et_ref[pl.ds(r0, K), pl.ds(lo, LANE)] = be
            gt_ref[pl.ds(r0, K), pl.ds(lo, LANE)] = g
            return c2

        lax.fori_loop(0, nchunk, chunk_body, 0)
        return carry

    lax.fori_loop(0, PEER_HEADS, head_body, 0)


def _outproj(h, yl, o, p, *, tm):
    B, Tp, D = h.shape
    nb = Tp // tm
    H = MLA_HEADS
    HK = PEER_HEADS * PEER_TOPK
    full = lambda a: pl.BlockSpec(a.shape, lambda b, i: (0,) * a.ndim)
    tok = lambda w: pl.BlockSpec((None, tm, w), lambda b, i: (b, i, 0))
    sel = pl.BlockSpec((HK, tm), lambda b, i: (0, b * nb + i))
    ws = [p["att_g"], p["w_out_lru"], p["w_out_att"], p["ffn_g"], p["peer_wq"], p["subkeys"]]
    return pl.pallas_call(
        functools.partial(_outproj_kernel, tm=tm),
        grid=(B, nb),
        in_specs=[tok(D), tok(LRU_WIDTH), tok(H * V_DIM)] + [full(w) for w in ws],
        out_specs=[tok(D), tok(D), sel, sel],
        out_shape=[jax.ShapeDtypeStruct((B, Tp, D), F32), jax.ShapeDtypeStruct((B, Tp, D), BF16),
                   jax.ShapeDtypeStruct((HK, B * Tp), jnp.int32), jax.ShapeDtypeStruct((HK, B * Tp), F32)],
        scratch_shapes=[pltpu.VMEM((tm, PEER_HEADS * 2 * D_SUBKEY), F32), pltpu.VMEM((2, N_KEYS, tm), F32)],
        compiler_params=_params("parallel", "parallel"),
        name="outproj_topk",
    )(h, yl, o, *ws)


def _build_slot_tables(c, et_ref, gt_ref, tabi_ref, tabg_ref, eperm_ref, gov_ref, rcnt_ref, *, rcap, rs):
    HK = PEER_HEADS * PEER_TOPK
    NV = HK // 8
    lo = pl.multiple_of(c * LANE, LANE)
    ex = et_ref[:, pl.ds(lo, LANE)]
    gx = gt_ref[:, pl.ds(lo, LANE)]
    ik = ex >> 7
    dk = (ik + (ex & (N_KEYS - 1))) & (N_KEYS - 1)
    dv = [dk[v * 8:(v + 1) * 8, :] for v in range(NV)]
    rank = [jnp.zeros((8, LANE), jnp.int32) for _ in range(NV)]
    sub = lax.broadcasted_iota(jnp.int32, (8, LANE), 0)
    for kp in range(HK - 1):
        v0, r0 = divmod(kp, 8)
        row = dv[v0][r0:r0 + 1, :]
        if r0 < 7:
            rank[v0] = rank[v0] + jnp.where(sub > r0, jnp.where(dv[v0] == row, 1, 0), 0)
        for v in range(v0 + 1, NV):
            rank[v] = rank[v] + jnp.where(dv[v] == row, 1, 0)
    rank = jnp.concatenate(rank, axis=0)
    key = rank * N_KEYS + dk
    rcnt_ref[c] = jnp.max(rank) + 1
    eperm_ref[:, pl.ds(lo, LANE)] = ik * N_KEYS + dk
    gov_ref[:, pl.ds(lo, LANE)] = jnp.where(rank >= rcap, gx, 0.0)
    dio = lax.broadcasted_iota(jnp.int32, (N_KEYS, LANE), 0)

    def round_body(r, carry):
        want = dio + r * N_KEYS
        ti = jnp.full((N_KEYS, LANE), -1, jnp.int32)
        tg = jnp.zeros((N_KEYS, LANE), F32)
        for k in range(HK):
            hit = key[k:k + 1, :] == want
            ti = jnp.where(hit, ik[k:k + 1, :], ti)
            tg = jnp.where(hit, gx[k:k + 1, :], tg)
        r0 = pl.multiple_of(r * N_KEYS, N_KEYS)
        tabi_ref[pl.ds(r0, N_KEYS), pl.ds(lo, LANE)] = ti
        tabg_ref[pl.ds(r0, N_KEYS), pl.ds(lo, LANE)] = tg
        return carry

    nbuilt = jnp.minimum(rcnt_ref[c], rcap)
    lax.fori_loop(0, nbuilt, round_body, 0)

    def empty_body(r, carry):
        r0 = pl.multiple_of(r * N_KEYS, N_KEYS)
        tabi_ref[pl.ds(r0, N_KEYS), pl.ds(lo, LANE)] = jnp.full((N_KEYS, LANE), -1, jnp.int32)
        tabg_ref[pl.ds(r0, N_KEYS), pl.ds(lo, LANE)] = jnp.zeros((N_KEYS, LANE), F32)
        return carry

    lax.fori_loop(nbuilt, rs, empty_body, 0)


def _peer_kernel(hn_ref, et_ref, gt_ref, u_ref, vt_ref, h2_ref, out_ref, pre_ref, act_ref, gx_ref, acc_ref,
                 tabi_ref, tabg_ref, eperm_ref, gov_ref, rcnt_ref, *, tm, te, rcap, rs):
    e = pl.program_id(1)
    HK = PEER_HEADS * PEER_TOPK
    nchunk = tm // LANE
    ntile = te // N_KEYS
    i_base = (e - 1) * ntile

    @pl.when(e == 0)
    def _():
        acc_ref[...] = jnp.zeros(acc_ref.shape, F32)
        pre_ref[...] = jnp.zeros(pre_ref.shape, F32)
        gx_ref[...] = jnp.zeros(gx_ref.shape, F32)

        def build(c, carry):
            _build_slot_tables(c, et_ref, gt_ref, tabi_ref, tabg_ref, eperm_ref, gov_ref, rcnt_ref,
                               rcap=rcap, rs=rs)
            return carry

        lax.fori_loop(0, nchunk, build, 0)

    def deep_chunk(c, carry):
        nround = rcnt_ref[c]

        @pl.when(nround > rs)
        def _():
            lo = pl.multiple_of(c * LANE, LANE)
            ntab = jnp.minimum(nround, rcap)
            nover = jnp.where(nround > rcap, HK // 8, 0)

            def tile_body(it, c2):
                i_abs = i_base + it
                eio = i_abs * N_KEYS + lax.broadcasted_iota(jnp.int32, (N_KEYS, LANE), 0)

                def round_body(r, g):
                    t0 = pl.multiple_of(r * N_KEYS, N_KEYS)
                    ti = tabi_ref[pl.ds(t0, N_KEYS), pl.ds(lo, LANE)]
                    tg = tabg_ref[pl.ds(t0, N_KEYS), pl.ds(lo, LANE)]
                    return g + jnp.where(ti == i_abs, tg, 0.0)

                def over_body(k8, g):
                    k0 = pl.multiple_of(k8 * 8, 8)
                    eb = eperm_ref[pl.ds(k0, 8), pl.ds(lo, LANE)]
                    gb = gov_ref[pl.ds(k0, 8), pl.ds(lo, LANE)]
                    for r in range(8):
                        g = g + jnp.where(eb[r:r + 1, :] == eio, gb[r:r + 1, :], 0.0)
                    return g

                g = lax.fori_loop(rs, ntab, round_body, jnp.zeros((N_KEYS, LANE), F32))
                g = lax.fori_loop(0, nover, over_body, g)
                gx_ref[pl.ds(pl.multiple_of(it * N_KEYS, N_KEYS), N_KEYS), pl.ds(lo, LANE)] = g
                return c2

            lax.fori_loop(0, ntile, tile_body, 0)

        return carry

    lax.fori_loop(0, nchunk, deep_chunk, 0)

    SG = 32
    for c in range(nchunk):
        lanes = slice(c * LANE, (c + 1) * LANE)
        for sg in range(N_KEYS // SG):
            g = [gx_ref[it * N_KEYS + sg * SG:it * N_KEYS + (sg + 1) * SG, lanes] for it in range(ntile)]
            for r in range(rs):
                trow = slice(r * N_KEYS + sg * SG, r * N_KEYS + (sg + 1) * SG)
                ti = tabi_ref[trow, lanes]
                tg = tabg_ref[trow, lanes]
                for it in range(ntile):
                    g[it] = g[it] + jnp.where(ti == i_base + it, tg, 0.0)
            for it in range(ntile):
                rows = slice(it * N_KEYS + sg * SG, it * N_KEYS + (sg + 1) * SG)
                act_ref[rows, lanes] = (_gelu(pre_ref[rows, lanes]) * g[it]).astype(BF16)
    acc_ref[...] += _dot(vt_ref[...], act_ref[...])
    pre_ref[...] = _dot_nt(u_ref[...], hn_ref[...])

    @pl.when(e == pl.num_programs(1) - 1)
    def _():
        out_ref[...] = h2_ref[...] + acc_ref[...].T


def _peer(h2, hn, et, gt, u, vt, *, tm, te, rcap=16, rs=6):
    N, D = h2.shape
    E = u.shape[0]
    HK = et.shape[0]
    ne = E // te
    assert rs <= rcap
    tok = pl.BlockSpec((tm, D), lambda i, e: (i, 0))
    sel = pl.BlockSpec((HK, tm), lambda i, e: (0, i))
    return pl.pallas_call(
        functools.partial(_peer_kernel, tm=tm, te=te, rcap=rcap, rs=rs),
        grid=(N // tm, ne + 1),
        in_specs=[tok, sel, sel, pl.BlockSpec((te, D), lambda i, e: (jnp.minimum(e, ne - 1), 0)),
                  pl.BlockSpec((D, te), lambda i, e: (0, jnp.maximum(e - 1, 0))), tok],
        out_specs=tok,
        out_shape=jax.ShapeDtypeStruct((N, D), F32),
        scratch_shapes=[pltpu.VMEM((te, tm), F32), pltpu.VMEM((te, tm), BF16), pltpu.VMEM((te, tm), F32),
                        pltpu.VMEM((D, tm), F32),
                        pltpu.VMEM((rcap * N_KEYS, tm), jnp.int32), pltpu.VMEM((rcap * N_KEYS, tm), F32),
                        pltpu.VMEM((HK, tm), jnp.int32), pltpu.VMEM((HK, tm), F32),
                        pltpu.SMEM((tm // LANE,), jnp.int32)],
        compiler_params=_params("parallel", "arbitrary"),
        name="peer_experts",
    )(hn, et, gt, u, vt, h2)


def _diag_experts(w):
    D = w.shape[-1]
    i = jnp.arange(N_KEYS)[:, None]
    d = jnp.arange(N_KEYS)[None, :]
    j = (d - i) % N_KEYS
    return jnp.take_along_axis(w.reshape(N_KEYS, N_KEYS, D), j[:, :, None], axis=1).reshape(N_KEYS * N_KEYS, D)


def _slots(w, width):
    lead = w.shape[:-1]
    w = w.reshape(*lead, -1, width)
    w = jnp.pad(w, [(0, 0)] * len(lead) + [(0, 0), (0, SLOT - width)])
    return w.reshape(*lead, -1)


def _block_diag(w):
    n, hd, _ = w.shape
    eye = jnp.eye(n, dtype=w.dtype)
    return (eye[:, None, :, None] * w[:, :, None, :]).reshape(n * hd, n * hd)


def _rope_tables(Tp, n_pad):
    half = QK_ROPE // 2
    pos = (jnp.arange(Tp) - n_pad).astype(F32)
    freq = ROPE_THETA ** (-jnp.arange(half, dtype=F32) / half)
    ang = pos[:, None] * freq[None, :]
    cos, sin = jnp.cos(ang), jnp.sin(ang)
    z = lambda n: jnp.zeros((Tp, n), F32)
    tail = SLOT - QK_DIM
    rc = jnp.concatenate([jnp.ones((Tp, QK_NOPE), F32), cos, cos, jnp.ones((Tp, tail), F32)], axis=1)
    rs1 = jnp.concatenate([z(QK_NOPE), -sin, z(half), z(tail)], axis=1)
    rs2 = jnp.concatenate([z(QK_NOPE), z(half), sin, z(tail)], axis=1)
    return rc, rs1, rs2


def _layer_params(l, mix_norm_g, w_in, conv_w, conv_b, lru_wa, lru_ba, lru_wx, lru_bx, lru_lambda,
                  q_lora_norm_g, w_uq, kv_lora_norm_g, w_ukv, q_head_norm_g, k_head_norm_g, lru_out_norm_g,
                  attn_out_norm_g, w_out, ffn_norm_g, peer_wq, peer_subkeys, peer_u, peer_v):
    H = MLA_HEADS
    row = lambda a: a[None, :].astype(F32)
    wi = w_in[l]
    split4 = 2 * LRU_WIDTH + Q_LORA + KV_LORA
    kr_cols = jnp.pad(wi[:, split4:], ((0, 0), (QK_NOPE, SLOT - QK_DIM)))
    wkv = w_ukv[l].reshape(KV_LORA, H, QK_NOPE + V_DIM)
    return {
        "mix_g": row(mix_norm_g[l]),
        "w_in": jnp.concatenate([wi[:, :split4], kr_cols], axis=1).astype(BF16),
        "q_g": row(q_lora_norm_g[l]),
        "w_uq": _slots(w_uq[l], QK_DIM).astype(BF16),
        "kv_g": row(kv_lora_norm_g[l]),
        "w_uk": _slots(wkv[:, :, :QK_NOPE].reshape(KV_LORA, H * QK_NOPE), QK_NOPE).astype(BF16),
        "w_uv": _slots(wkv[:, :, QK_NOPE:].reshape(KV_LORA, H * V_DIM), V_DIM).astype(BF16),
        "qh_g": _slots(row(q_head_norm_g[l]), QK_DIM),
        "kh_g": _slots(row(k_head_norm_g[l]), QK_DIM),
        "conv_w": conv_w[l],
        "conv_b": row(conv_b[l]),
        "wa": _block_diag(lru_wa[l]).astype(BF16),
        "ba": row(lru_ba[l].reshape(-1)),
        "wx": _block_diag(lru_wx[l]).astype(BF16),
        "bx": row(lru_bx[l].reshape(-1)),
        "lam": row(lru_lambda[l]),
        "lru_og": row(lru_out_norm_g[l]),
        "att_g": row(attn_out_norm_g[l]),
        "w_out_lru": w_out[l][:LRU_WIDTH].astype(BF16),
        "w_out_att": w_out[l][LRU_WIDTH:].astype(BF16),
        "ffn_g": row(ffn_norm_g[l]),
        "peer_wq": peer_wq[l].astype(BF16),
        "subkeys": peer_subkeys[l].astype(BF16),
        "u": _diag_experts(peer_u[l].astype(BF16)),
        "vt": _diag_experts(peer_v[l].astype(BF16)).T,
    }


def _tiles(B, Tp):
    nblk = Tp // ATTN_BLOCK
    per_tok = ATTN_BLOCK * max(d for d in range(1, 6) if nblk % d == 0)
    n = B * nblk
    flat = ATTN_BLOCK * next(d for d in (4, 2, 5, 3, 1) if n % d == 0)
    return per_tok, flat


def kernel(x, meta_tokens, mix_norm_g, w_in, conv_w, conv_b, lru_wa, lru_ba, lru_wx, lru_bx, lru_lambda,
           q_lora_norm_g, w_uq, kv_lora_norm_g, w_ukv, q_head_norm_g, k_head_norm_g, lru_out_norm_g,
           attn_out_norm_g, w_out, ffn_norm_g, peer_wq, peer_subkeys, peer_u, peer_v):
    B, S, D = x.shape
    depth = w_in.shape[0]
    T = N_META + S
    n_pad = (-T) % ATTN_BLOCK
    Tp = T + n_pad
    tm, tflat = _tiles(B, Tp)
    meta = jnp.broadcast_to(meta_tokens.astype(x.dtype)[None], (B, N_META, D))
    h = jnp.concatenate([jnp.zeros((B, n_pad, D), x.dtype), meta, x], axis=1)
    rope_tabs = _rope_tables(Tp, n_pad)
    weights = (mix_norm_g, w_in, conv_w, conv_b, lru_wa, lru_ba, lru_wx, lru_bx, lru_lambda, q_lora_norm_g,
               w_uq, kv_lora_norm_g, w_ukv, q_head_norm_g, k_head_norm_g, lru_out_norm_g, attn_out_norm_g,
               w_out, ffn_norm_g, peer_wq, peer_subkeys, peer_u, peer_v)
    for l in range(depth):
        p = _layer_params(l, *weights)
        xl, gg, q, k, v = _inproj(h, p, rope_tabs, tm=tm, n_pad=n_pad)
        yl = _lru(xl, gg, p, tt=tm, n_pad=n_pad)
        o = _attention(q, k, v, tq=tm, tk=tm)
        h2, hn, et, gt = _outproj(h, yl, o, p, tm=tm)
        h = _peer(h2.reshape(B * Tp, D), hn.reshape(B * Tp, D), et, gt, p["u"], p["vt"],
                  tm=tflat, te=1024).reshape(B, Tp, D)
    return h[:, n_pad:][:, N_META:]
```

```python
import functools
import math

import jax
import jax.numpy as jnp
from jax import lax
from jax.experimental import pallas as pl
from jax.experimental.pallas import tpu as pltpu

N_META = 16
LRU_WIDTH = 512
LRU_HEADS = 8
LRU_HEAD_DIM = LRU_WIDTH // LRU_HEADS
CONV_WIDTH = 4
LRU_C = 8.0
MLA_HEADS = 8
Q_LORA = 384
KV_LORA = 256
QK_NOPE = 64
QK_ROPE = 32
QK_DIM = QK_NOPE + QK_ROPE
V_DIM = 64
ROPE_THETA = 10000.0
ATTN_BLOCK = 128
PEER_HEADS = 8
N_KEYS = 128
PEER_TOPK = 16
D_SUBKEY = 128
EPS = 1e-6
NEG_INF = -1e30

LANE = 128
SLOT = LANE
MASK_LANE = QK_DIM
ONES_LANE = V_DIM
M_INIT = -3.0e38
VMEM_LIMIT = 56 * 1024 * 1024

F32 = jnp.float32
BF16 = jnp.bfloat16
INV_SQRT2 = 1.0 / math.sqrt(2.0)


def _gelu(x):
    return 0.5 * x * (1.0 + lax.erf(x * INV_SQRT2))


def _rms(x, g, n=None):
    n = x.shape[-1] if n is None else n
    ms = jnp.sum(x * x, axis=-1, keepdims=True) * (1.0 / n)
    return x * lax.rsqrt(ms + EPS) * g


def _dot(a, b):
    return jnp.dot(a, b, preferred_element_type=F32)


def _dot_nt(a, b):
    return lax.dot_general(a, b, (((1,), (1,)), ((), ())), preferred_element_type=F32)


def _params(*sem):
    return pltpu.CompilerParams(dimension_semantics=sem, vmem_limit_bytes=VMEM_LIMIT)


def _inproj_kernel(h_ref, g_ref, win_ref, qg_ref, wuq_ref, kvg_ref, wuk_ref, wuv_ref, qhg_ref, khg_ref,
                   rc_ref, rs1_ref, rs2_ref, xl_ref, gg_ref, q_ref, k_ref, v_ref, *, tm, n_pad):
    h = h_ref[...]
    hn = _rms(h, g_ref[...])
    z = _dot(hn.astype(BF16), win_ref[...])
    o1 = LRU_WIDTH
    o2 = 2 * LRU_WIDTH
    o3 = o2 + Q_LORA
    o4 = o3 + KV_LORA
    xl_ref[...] = z[:, :o1]
    gg_ref[...] = _gelu(z[:, o1:o2])
    cqn = _rms(z[:, o2:o3], qg_ref[...]).astype(BF16)
    ckvn = _rms(z[:, o3:o4], kvg_ref[...]).astype(BF16)
    kr = z[:, o4:o4 + SLOT]
    q = _dot(cqn, wuq_ref[...])
    kn = _dot(ckvn, wuk_ref[...])
    v = _dot(ckvn, wuv_ref[...])
    rc = rc_ref[...]
    rs1 = rs1_ref[...]
    rs2 = rs2_ref[...]
    half = QK_ROPE // 2
    lane = lax.broadcasted_iota(jnp.int32, (1, SLOT), 1)
    mask_lane = (lane == MASK_LANE).astype(F32)
    ones_lane = (lane == ONES_LANE).astype(F32)
    row = pl.program_id(1) * tm + lax.broadcasted_iota(jnp.int32, (tm, 1), 0)
    kbias = jnp.where(row < n_pad, NEG_INF, 0.0) * mask_lane
    scale = 1.0 / math.sqrt(QK_DIM)

    def rope(x):
        return x * rc + pltpu.roll(x, SLOT - half, 1) * rs1 + pltpu.roll(x, half, 1) * rs2

    for hd in range(MLA_HEADS):
        sl = slice(hd * SLOT, (hd + 1) * SLOT)
        qs = rope(_rms(q[:, sl], qhg_ref[...], QK_DIM))
        q_ref[hd] = (qs * scale + mask_lane).astype(BF16)
        ks = rope(_rms(kn[:, sl] + kr, khg_ref[...], QK_DIM))
        k_ref[hd] = (ks + kbias).T.astype(BF16)
        v_ref[hd] = (v[:, sl] + ones_lane).astype(BF16)


def _inproj(h, p, rope_tabs, *, tm, n_pad):
    B, Tp, D = h.shape
    nb = Tp // tm
    H = MLA_HEADS
    full = lambda a: pl.BlockSpec(a.shape, lambda b, i: (0,) * a.ndim)
    tok = lambda w: pl.BlockSpec((None, tm, w), lambda b, i: (b, i, 0))
    head = pl.BlockSpec((None, H, tm, SLOT), lambda b, i: (b, 0, i, 0))
    tab = pl.BlockSpec((tm, SLOT), lambda b, i: (i, 0))
    ws = [p["mix_g"], p["w_in"], p["q_g"], p["w_uq"], p["kv_g"], p["w_uk"], p["w_uv"], p["qh_g"], p["kh_g"]]
    return pl.pallas_call(
        functools.partial(_inproj_kernel, tm=tm, n_pad=n_pad),
        grid=(B, nb),
        in_specs=[tok(D)] + [full(w) for w in ws] + [tab, tab, tab],
        out_specs=[tok(LRU_WIDTH), tok(LRU_WIDTH), head,
                   pl.BlockSpec((None, H, SLOT, tm), lambda b, i: (b, 0, 0, i)), head],
        out_shape=[jax.ShapeDtypeStruct((B, Tp, LRU_WIDTH), F32), jax.ShapeDtypeStruct((B, Tp, LRU_WIDTH), F32),
                   jax.ShapeDtypeStruct((B, H, Tp, SLOT), BF16), jax.ShapeDtypeStruct((B, H, SLOT, Tp), BF16),
                   jax.ShapeDtypeStruct((B, H, Tp, SLOT), BF16)],
        compiler_params=_params("parallel", "parallel"),
        name="inproj",
    )(h, *ws, *rope_tabs)


def _lru_kernel(xl_ref, gg_ref, cw_ref, cb_ref, wa_ref, ba_ref, wx_ref, bx_ref, lam_ref, og_ref, y_ref,
                ext_ref, hst_ref, *, tt, n_pad):
    t = pl.program_id(1)
    HALO = 8

    @pl.when(t == 0)
    def _():
        ext_ref[0:HALO, :] = jnp.zeros((HALO, LRU_WIDTH), F32)
        hst_ref[...] = jnp.zeros((HALO, LRU_WIDTH), F32)

    row = t * tt + lax.broadcasted_iota(jnp.int32, (tt, 1), 0)
    real = row >= n_pad
    ext_ref[HALO:HALO + tt, :] = jnp.where(real, xl_ref[...], 0.0)
    xc = cb_ref[...]
    for k in range(CONV_WIDTH):
        off = HALO - (CONV_WIDTH - 1) + k
        xc = xc + ext_ref[off:off + tt, :] * cw_ref[k:k + 1, :]
    xcb = xc.astype(BF16)
    r = jax.nn.sigmoid(_dot(xcb, wa_ref[...]) + ba_ref[...])
    i = jax.nn.sigmoid(_dot(xcb, wx_ref[...]) + bx_ref[...])
    nl = -lam_ref[...]
    softplus = jnp.maximum(nl, 0.0) + jnp.log1p(jnp.exp(-jnp.abs(nl)))
    log_a = -LRU_C * r * softplus
    a = jnp.exp(log_a)
    mult = jnp.sqrt(1.0 - jnp.exp(2.0 * log_a))
    b = jnp.where(real, mult * (i * xc), 0.0)
    ridx = lax.broadcasted_iota(jnp.int32, (tt, 1), 0)
    s = 1
    while s < tt:
        keep = ridx >= s
        a_sh = jnp.where(keep, pltpu.roll(a, s, 0), 1.0)
        b_sh = jnp.where(keep, pltpu.roll(b, s, 0), 0.0)
        b = a * b_sh + b
        a = a * a_sh
        s *= 2
    hprev = hst_ref[HALO - 1:HALO, :]
    hs = a * hprev + b
    hst_ref[...] = hs[tt - HALO:tt, :]
    ext_ref[0:HALO, :] = ext_ref[tt:tt + HALO, :]
    y = hs * gg_ref[...]
    y_ref[...] = _rms(y, og_ref[...]).astype(BF16)


def _lru(xl, gg, p, *, tt, n_pad):
    B, Tp, W = xl.shape
    full = lambda a: pl.BlockSpec(a.shape, lambda b, i: (0,) * a.ndim)
    tok = pl.BlockSpec((None, tt, W), lambda b, i: (b, i, 0))
    ws = [p["conv_w"], p["conv_b"], p["wa"], p["ba"], p["wx"], p["bx"], p["lam"], p["lru_og"]]
    return pl.pallas_call(
        functools.partial(_lru_kernel, tt=tt, n_pad=n_pad),
        grid=(B, Tp // tt),
        in_specs=[tok, tok] + [full(w) for w in ws],
        out_specs=tok,
        out_shape=jax.ShapeDtypeStruct((B, Tp, W), BF16),
        scratch_shapes=[pltpu.VMEM((tt + 8, W), F32), pltpu.VMEM((8, W), F32)],
        compiler_params=_params("parallel", "arbitrary"),
        name="rglru",
    )(xl, gg, *ws)


HEADS_PER_STEP = SLOT // V_DIM


def _attn_kernel(q_ref, k_ref, v_ref, o_ref, *, tq, tk):
    qi = pl.program_id(2)
    kd = (qi * tq) // tk

    def step(kb, carry, diagonal):
        start = pl.multiple_of(kb * tk, tk)
        out = []
        for hh in range(HEADS_PER_STEP):
            m, acc = carry[hh]
            s = _dot(q_ref[hh], k_ref[hh, :, pl.ds(start, tk)])
            if diagonal:
                qpos = qi * tq + lax.broadcasted_iota(jnp.int32, (tq, tk), 0)
                kpos = kb * tk + lax.broadcasted_iota(jnp.int32, (tq, tk), 1)
                s = jnp.where(kpos <= qpos, s, NEG_INF)
            m_new = jnp.maximum(m, jnp.max(s, axis=-1, keepdims=True))
            alpha = jnp.exp(m - m_new)
            pr = jnp.exp(s - m_new).astype(BF16)
            out.append((m_new, acc * alpha + _dot(pr, v_ref[hh, pl.ds(start, tk), :])))
        return tuple(out)

    init = tuple((jnp.full((tq, 1), M_INIT, F32), jnp.zeros((tq, SLOT), F32)) for _ in range(HEADS_PER_STEP))
    carry = lax.fori_loop(0, kd, lambda kb, c: step(kb, c, False), init)
    carry = step(kd, carry, True)
    lane = lax.broadcasted_iota(jnp.int32, (1, SLOT), 1)
    o = jnp.zeros((tq, SLOT), F32)
    for hh in range(HEADS_PER_STEP):
        acc = carry[hh][1]
        oh = acc / acc[:, ONES_LANE:ONES_LANE + 1]
        if hh:
            oh = pltpu.roll(oh, hh * V_DIM, 1)
        o = jnp.where((lane >= hh * V_DIM) & (lane < (hh + 1) * V_DIM), oh, o)
    o_ref[...] = o.astype(BF16)


def _attention(q, k, v, *, tq, tk):
    B, H, Tp, _ = q.shape
    G = HEADS_PER_STEP
    qspec = pl.BlockSpec((None, G, tq, SLOT), lambda b, h, i: (b, h, i, 0))
    kvspec = pl.BlockSpec((None, G, Tp, SLOT), lambda b, h, i: (b, h, 0, 0))
    return pl.pallas_call(
        functools.partial(_attn_kernel, tq=tq, tk=tk),
        grid=(B, H // G, Tp // tq),
        in_specs=[qspec, pl.BlockSpec((None, G, SLOT, Tp), lambda b, h, i: (b, h, 0, 0)), kvspec],
        out_specs=pl.BlockSpec((None, tq, SLOT), lambda b, h, i: (b, i, h)),
        out_shape=jax.ShapeDtypeStruct((B, Tp, H * V_DIM), BF16),
        compiler_params=_params("parallel", "parallel", "arbitrary"),
        name="attention",
    )(q, k, v)


def _extract_top(x, n, payload=None):
    R, L = x.shape
    pos = lax.broadcasted_iota(jnp.int32, (R, L), 0)
    out_row = lax.broadcasted_iota(jnp.int32, (n, L), 0)
    vals = jnp.zeros((n, L), F32)
    idxs = jnp.zeros((n, L), jnp.int32)
    for it in range(n):
        m = jnp.max(x, axis=0, keepdims=True)
        idx = jnp.min(jnp.where(x == m, pos, R), axis=0, keepdims=True)
        hit = pos == idx
        if payload is None:
            sel = idx
        else:
            sel = jnp.sum(jnp.where(hit, payload, 0), axis=0, keepdims=True)
        vals = jnp.where(out_row == it, m, vals)
        idxs = jnp.where(out_row == it, sel, idxs)
        x = jnp.where(hit, -jnp.inf, x)
    return vals, idxs


def _merge_sort_network(n):
    pairs = []

    def merge(lo, cnt, r):
        step = r * 2
        if step < cnt:
            merge(lo, cnt, step)
            merge(lo + r, cnt, step)
            pairs.extend((i, i + r) for i in range(lo + r, lo + cnt - r, step))
        else:
            pairs.append((lo, lo + r))

    def sort(lo, cnt):
        if cnt > 1:
            sort(lo, cnt // 2)
            sort(lo + cnt // 2, cnt // 2)
            merge(lo, cnt, 1)

    sort(0, n)
    return pairs


def _top_rows(x, n):
    R, L = x.shape
    nv = R // 8
    assert n <= nv
    sub = lax.broadcasted_iota(jnp.int32, (8, L), 0)
    val = [x[8 * v:8 * (v + 1), :] for v in range(nv)]
    row = [sub + 8 * v for v in range(nv)]
    for p, q in _merge_sort_network(nv):
        a, b, ia, ib = val[p], val[q], row[p], row[q]
        swap = (b > a) | ((b == a) & (ib < ia))
        val[p], val[q] = jnp.where(swap, b, a), jnp.where(swap, a, b)
        row[p], row[q] = jnp.where(swap, ib, ia), jnp.where(swap, ia, ib)
    out_row = lax.broadcasted_iota(jnp.int32, (n, L), 0)
    vals = jnp.zeros((n, L), F32)
    idxs = jnp.zeros((n, L), jnp.int32)
    for it in range(n):
        m = jnp.max(val[0], axis=0, keepdims=True)
        idx = jnp.min(jnp.where(val[0] == m, row[0], R), axis=0, keepdims=True)
        vals = jnp.where(out_row == it, m, vals)
        idxs = jnp.where(out_row == it, idx, idxs)
        pop = row[0] == idx
        for v in range(n - it - 1):
            val[v] = jnp.where(pop, val[v + 1], val[v])
            row[v] = jnp.where(pop, row[v + 1], row[v])
    return vals, idxs


def _candidates(ts0, ti0, ts1, ti1):
    K = PEER_TOPK
    sub = lax.broadcasted_iota(jnp.int32, (8, ts0.shape[1]), 0)
    cs = [ts0[0:1, :] + ts1]
    ci = [ti0[0:1, :] * N_KEYS + ti1]
    for a in range(1, 8):
        nb = K // (a + 1)
        piece = ts0[a:a + 1, :] + ts1[0:8, :]
        cs.append(piece if nb >= 8 else jnp.where(sub < nb, piece, -jnp.inf))
        ci.append(ti0[a:a + 1, :] * N_KEYS + ti1[0:8, :])
    cs.append(ts0[8:K, :] + ts1[0:1, :])
    ci.append(ti0[8:K, :] * N_KEYS + ti1[0:1, :])
    return jnp.concatenate(cs, axis=0), jnp.concatenate(ci, axis=0)


def _outproj_kernel(h_ref, yl_ref, o_ref, ag_ref, wol_ref, woa_ref, fg_ref, wq_ref, sk_ref,
                    h2_ref, hn_ref, et_ref, gt_ref, qp_ref, s_ref, *, tm):
    an = _rms(o_ref[...].astype(F32), ag_ref[...]).astype(BF16)
    h2 = h_ref[...] + _dot(yl_ref[...], wol_ref[...]) + _dot(an, woa_ref[...])
    h2_ref[...] = h2
    hn = _rms(h2, fg_ref[...]).astype(BF16)
    hn_ref[...] = hn
    qp_ref[...] = _dot(hn, wq_ref[...])
    K = PEER_TOPK
    nchunk = tm // LANE

    def head_body(hd, carry):
        for pp in range(2):
            off = pl.multiple_of((hd * 2 + pp) * D_SUBKEY, D_SUBKEY)
            qs = qp_ref[:, pl.ds(off, D_SUBKEY)].astype(BF16)
            s_ref[pp] = _dot_nt(sk_ref[pp], qs)

        r0 = pl.multiple_of(hd * K, K)
        for j in range(nchunk):
            lanes = slice(j * LANE, (j + 1) * LANE)
            ts0, ti0 = _top_rows(s_ref[0, :, lanes], K)
            ts1, ti1 = _top_rows(s_ref[1, :, lanes], K)
            cs, ci = _candidates(ts0, ti0, ts1, ti1)
            bs, be = _extract_top(cs, K, payload=ci)
            e = jnp.exp(bs - bs[0:1, :])
            et_ref[pl.ds(r0, K), lanes] = be
            gt_ref[pl.ds(r0, K), lanes] = e / jnp.sum(e, axis=0, keepdims=True)
        return carry

    lax.fori_loop(0, PEER_HEADS, head_body, 0)


def _outproj(h, yl, o, p, *, tm):
    B, Tp, D = h.shape
    nb = Tp // tm
    H = MLA_HEADS
    HK = PEER_HEADS * PEER_TOPK
    full = lambda a: pl.BlockSpec(a.shape, lambda b, i: (0,) * a.ndim)
    tok = lambda w: pl.BlockSpec((None, tm, w), lambda b, i: (b, i, 0))
    sel = pl.BlockSpec((HK, tm), lambda b, i: (0, b * nb + i))
    ws = [p["att_g"], p["w_out_lru"], p["w_out_att"], p["ffn_g"], p["peer_wq"], p["subkeys"]]
    return pl.pallas_call(
        functools.partial(_outproj_kernel, tm=tm),
        grid=(B, nb),
        in_specs=[tok(D), tok(LRU_WIDTH), tok(H * V_DIM)] + [full(w) for w in ws],
        out_specs=[tok(D), tok(D), sel, sel],
        out_shape=[jax.ShapeDtypeStruct((B, Tp, D), F32), jax.ShapeDtypeStruct((B, Tp, D), BF16),
                   jax.ShapeDtypeStruct((HK, B * Tp), jnp.int32), jax.ShapeDtypeStruct((HK, B * Tp), F32)],
        scratch_shapes=[pltpu.VMEM((tm, PEER_HEADS * 2 * D_SUBKEY), F32), pltpu.VMEM((2, N_KEYS, tm), F32)],
        compiler_params=_params("parallel", "parallel"),
        name="outproj_topk",
    )(h, yl, o, *ws)


def _build_slot_tables(c, et_ref, gt_ref, tabi_ref, tabg_ref, eperm_ref, gov_ref, rcnt_ref, *, rcap, rs):
    HK = PEER_HEADS * PEER_TOPK
    NV = HK // 8
    lo = pl.multiple_of(c * LANE, LANE)
    ex = et_ref[:, pl.ds(lo, LANE)]
    gx = gt_ref[:, pl.ds(lo, LANE)]
    ik = ex >> 7
    dk = (ik + (ex & (N_KEYS - 1))) & (N_KEYS - 1)
    dv = [dk[v * 8:(v + 1) * 8, :] for v in range(NV)]
    rank = [jnp.zeros((8, LANE), jnp.int32) for _ in range(NV)]
    sub = lax.broadcasted_iota(jnp.int32, (8, LANE), 0)
    for kp in range(HK - 1):
        v0, r0 = divmod(kp, 8)
        row = dv[v0][r0:r0 + 1, :]
        if r0 < 7:
            rank[v0] = rank[v0] + jnp.where(sub > r0, jnp.where(dv[v0] == row, 1, 0), 0)
        for v in range(v0 + 1, NV):
            rank[v] = rank[v] + jnp.where(dv[v] == row, 1, 0)
    rank = jnp.concatenate(rank, axis=0)
    key = rank * N_KEYS + dk
    rcnt_ref[c] = jnp.max(rank) + 1
    eperm_ref[:, pl.ds(lo, LANE)] = ik * N_KEYS + dk
    gov_ref[:, pl.ds(lo, LANE)] = jnp.where(rank >= rcap, gx, 0.0)
    dio = lax.broadcasted_iota(jnp.int32, (N_KEYS, LANE), 0)

    def round_body(r, carry):
        want = dio + r * N_KEYS
        ti = jnp.full((N_KEYS, LANE), -1, jnp.int32)
        tg = jnp.zeros((N_KEYS, LANE), F32)
        for k in range(HK):
            hit = key[k:k + 1, :] == want
            ti = jnp.where(hit, ik[k:k + 1, :], ti)
            tg = jnp.where(hit, gx[k:k + 1, :], tg)
        r0 = pl.multiple_of(r * N_KEYS, N_KEYS)
        tabi_ref[pl.ds(r0, N_KEYS), pl.ds(lo, LANE)] = ti
        tabg_ref[pl.ds(r0, N_KEYS), pl.ds(lo, LANE)] = tg
        return carry

    nbuilt = jnp.minimum(rcnt_ref[c], rcap)
    lax.fori_loop(0, nbuilt, round_body, 0)

    def empty_body(r, carry):
        r0 = pl.multiple_of(r * N_KEYS, N_KEYS)
        tabi_ref[pl.ds(r0, N_KEYS), pl.ds(lo, LANE)] = jnp.full((N_KEYS, LANE), -1, jnp.int32)
        tabg_ref[pl.ds(r0, N_KEYS), pl.ds(lo, LANE)] = jnp.zeros((N_KEYS, LANE), F32)
        return carry

    lax.fori_loop(nbuilt, rs, empty_body, 0)


def _peer_kernel(hn_ref, et_ref, gt_ref, u_ref, vt_ref, h2_ref, out_ref, pre_ref, act_ref, gx_ref, acc_ref,
                 tabi_ref, tabg_ref, eperm_ref, gov_ref, rcnt_ref, *, tm, te, rcap, rs):
    e = pl.program_id(1)
    HK = PEER_HEADS * PEER_TOPK
    nchunk = tm // LANE
    ntile = te // N_KEYS
    i_base = (e - 1) * ntile

    @pl.when(e == 0)
    def _():
        acc_ref[...] = jnp.zeros(acc_ref.shape, F32)
        pre_ref[...] = jnp.zeros(pre_ref.shape, F32)
        gx_ref[...] = jnp.zeros(gx_ref.shape, F32)

        def build(c, carry):
            _build_slot_tables(c, et_ref, gt_ref, tabi_ref, tabg_ref, eperm_ref, gov_ref, rcnt_ref,
                               rcap=rcap, rs=rs)
            return carry

        lax.fori_loop(0, nchunk, build, 0)

    def deep_chunk(c, carry):
        nround = rcnt_ref[c]

        @pl.when(nround > rs)
        def _():
            lo = pl.multiple_of(c * LANE, LANE)
            ntab = jnp.minimum(nround, rcap)
            nover = jnp.where(nround > rcap, HK // 8, 0)

            def tile_body(it, c2):
                i_abs = i_base + it
                eio = i_abs * N_KEYS + lax.broadcasted_iota(jnp.int32, (N_KEYS, LANE), 0)

                def round_body(r, g):
                    t0 = pl.multiple_of(r * N_KEYS, N_KEYS)
                    ti = tabi_ref[pl.ds(t0, N_KEYS), pl.ds(lo, LANE)]
                    tg = tabg_ref[pl.ds(t0, N_KEYS), pl.ds(lo, LANE)]
                    return g + jnp.where(ti == i_abs, tg, 0.0)

                def over_body(k8, g):
                    k0 = pl.multiple_of(k8 * 8, 8)
                    eb = eperm_ref[pl.ds(k0, 8), pl.ds(lo, LANE)]
                    gb = gov_ref[pl.ds(k0, 8), pl.ds(lo, LANE)]
                    for r in range(8):
                        g = g + jnp.where(eb[r:r + 1, :] == eio, gb[r:r + 1, :], 0.0)
                    return g

                g = lax.fori_loop(rs, ntab, round_body, jnp.zeros((N_KEYS, LANE), F32))
                g = lax.fori_loop(0, nover, over_body, g)
                gx_ref[pl.ds(pl.multiple_of(it * N_KEYS, N_KEYS), N_KEYS), pl.ds(lo, LANE)] = g
                return c2

            lax.fori_loop(0, ntile, tile_body, 0)

        return carry

    lax.fori_loop(0, nchunk, deep_chunk, 0)

    SG = 32
    for c in range(nchunk):
        lanes = slice(c * LANE, (c + 1) * LANE)
        for sg in range(N_KEYS // SG):
            g = [gx_ref[it * N_KEYS + sg * SG:it * N_KEYS + (sg + 1) * SG, lanes] for it in range(ntile)]
            for r in range(rs):
                trow = slice(r * N_KEYS + sg * SG, r * N_KEYS + (sg + 1) * SG)
                ti = tabi_ref[trow, lanes]
                tg = tabg_ref[trow, lanes]
                for it in range(ntile):
                    g[it] = g[it] + jnp.where(ti == i_base + it, tg, 0.0)
            for it in range(ntile):
                rows = slice(it * N_KEYS + sg * SG, it * N_KEYS + (sg + 1) * SG)
                act_ref[rows, lanes] = (_gelu(pre_ref[rows, lanes]) * g[it]).astype(BF16)
    acc_ref[...] += _dot(vt_ref[...], act_ref[...])
    pre_ref[...] = _dot_nt(u_ref[...], hn_ref[...])

    @pl.when(e == pl.num_programs(1) - 1)
    def _():
        out_ref[...] = h2_ref[...] + acc_ref[...].T


def _peer(h2, hn, et, gt, u, vt, *, tm, te, rcap=16, rs=6):
    N, D = h2.shape
    E = u.shape[0]
    HK = et.shape[0]
    ne = E // te
    assert rs <= rcap
    tok = pl.BlockSpec((tm, D), lambda i, e: (i, 0))
    sel = pl.BlockSpec((HK, tm), lambda i, e: (0, i))
    return pl.pallas_call(
        functools.partial(_peer_kernel, tm=tm, te=te, rcap=rcap, rs=rs),
        grid=(N // tm, ne + 1),
        in_specs=[tok, sel, sel, pl.BlockSpec((te, D), lambda i, e: (jnp.minimum(e, ne - 1), 0)),
                  pl.BlockSpec((D, te), lambda i, e: (0, jnp.maximum(e - 1, 0))), tok],
        out_specs=tok,
        out_shape=jax.ShapeDtypeStruct((N, D), F32),
        scratch_shapes=[pltpu.VMEM((te, tm), F32), pltpu.VMEM((te, tm), BF16), pltpu.VMEM((te, tm), F32),
                        pltpu.VMEM((D, tm), F32),
                        pltpu.VMEM((rcap * N_KEYS, tm), jnp.int32), pltpu.VMEM((rcap * N_KEYS, tm), F32),
                        pltpu.VMEM((HK, tm), jnp.int32), pltpu.VMEM((HK, tm), F32),
                        pltpu.SMEM((tm // LANE,), jnp.int32)],
        compiler_params=_params("parallel", "arbitrary"),
        name="peer_experts",
    )(hn, et, gt, u, vt, h2)


def _diag_experts(w):
    n, D = N_KEYS, w.shape[-1]
    w2 = jnp.concatenate([w.reshape(n, n, D)] * 2, axis=1).reshape(2 * n * n, D)
    return w2[n:n + n * (2 * n - 1)].reshape(n, 2 * n - 1, D)[:, :n].reshape(n * n, D)


def _slots(w, width):
    lead = w.shape[:-1]
    w = w.reshape(*lead, -1, width)
    w = jnp.pad(w, [(0, 0)] * len(lead) + [(0, 0), (0, SLOT - width)])
    return w.reshape(*lead, -1)


def _block_diag(w):
    n, hd, _ = w.shape
    eye = jnp.eye(n, dtype=w.dtype)
    return (eye[:, None, :, None] * w[:, :, None, :]).reshape(n * hd, n * hd)


def _rope_tables(Tp, n_pad):
    half = QK_ROPE // 2
    pos = (jnp.arange(Tp) - n_pad).astype(F32)
    freq = ROPE_THETA ** (-jnp.arange(half, dtype=F32) / half)
    ang = pos[:, None] * freq[None, :]
    cos, sin = jnp.cos(ang), jnp.sin(ang)
    z = lambda n: jnp.zeros((Tp, n), F32)
    tail = SLOT - QK_DIM
    rc = jnp.concatenate([jnp.ones((Tp, QK_NOPE), F32), cos, cos, jnp.ones((Tp, tail), F32)], axis=1)
    rs1 = jnp.concatenate([z(QK_NOPE), -sin, z(half), z(tail)], axis=1)
    rs2 = jnp.concatenate([z(QK_NOPE), z(half), sin, z(tail)], axis=1)
    return rc, rs1, rs2


def _layer_params(l, mix_norm_g, w_in, conv_w, conv_b, lru_wa, lru_ba, lru_wx, lru_bx, lru_lambda,
                  q_lora_norm_g, w_uq, kv_lora_norm_g, w_ukv, q_head_norm_g, k_head_norm_g, lru_out_norm_g,
                  attn_out_norm_g, w_out, ffn_norm_g, peer_wq, peer_subkeys, peer_u, peer_v):
    H = MLA_HEADS
    row = lambda a: a[None, :].astype(F32)
    wi = w_in[l]
    split4 = 2 * LRU_WIDTH + Q_LORA + KV_LORA
    kr_cols = jnp.pad(wi[:, split4:], ((0, 0), (QK_NOPE, SLOT - QK_DIM)))
    wkv = w_ukv[l].reshape(KV_LORA, H, QK_NOPE + V_DIM)
    return {
        "mix_g": row(mix_norm_g[l]),
        "w_in": jnp.concatenate([wi[:, :split4], kr_cols], axis=1).astype(BF16),
        "q_g": row(q_lora_norm_g[l]),
        "w_uq": _slots(w_uq[l], QK_DIM).astype(BF16),
        "kv_g": row(kv_lora_norm_g[l]),
        "w_uk": _slots(wkv[:, :, :QK_NOPE].reshape(KV_LORA, H * QK_NOPE), QK_NOPE).astype(BF16),
        "w_uv": _slots(wkv[:, :, QK_NOPE:].reshape(KV_LORA, H * V_DIM), V_DIM).astype(BF16),
        "qh_g": _slots(row(q_head_norm_g[l]), QK_DIM),
        "kh_g": _slots(row(k_head_norm_g[l]), QK_DIM),
        "conv_w": conv_w[l],
        "conv_b": row(conv_b[l]),
        "wa": _block_diag(lru_wa[l]).astype(BF16),
        "ba": row(lru_ba[l].reshape(-1)),
        "wx": _block_diag(lru_wx[l]).astype(BF16),
        "bx": row(lru_bx[l].reshape(-1)),
        "lam": row(lru_lambda[l]),
        "lru_og": row(lru_out_norm_g[l]),
        "att_g": row(attn_out_norm_g[l]),
        "w_out_lru": w_out[l][:LRU_WIDTH].astype(BF16),
        "w_out_att": w_out[l][LRU_WIDTH:].astype(BF16),
        "ffn_g": row(ffn_norm_g[l]),
        "peer_wq": peer_wq[l].astype(BF16),
        "subkeys": peer_subkeys[l].astype(BF16),
        "u": _diag_experts(peer_u[l].astype(BF16)),
        "vt": _diag_experts(peer_v[l].astype(BF16)).T,
    }


def _tiles(B, Tp):
    nblk = Tp // ATTN_BLOCK
    per_tok = ATTN_BLOCK * max(d for d in range(1, 6) if nblk % d == 0)
    n = B * nblk
    flat = ATTN_BLOCK * next(d for d in (4, 2, 5, 3, 1) if n % d == 0)
    return per_tok, flat


def kernel(x, meta_tokens, mix_norm_g, w_in, conv_w, conv_b, lru_wa, lru_ba, lru_wx, lru_bx, lru_lambda,
           q_lora_norm_g, w_uq, kv_lora_norm_g, w_ukv, q_head_norm_g, k_head_norm_g, lru_out_norm_g,
           attn_out_norm_g, w_out, ffn_norm_g, peer_wq, peer_subkeys, peer_u, peer_v):
    B, S, D = x.shape
    depth = w_in.shape[0]
    T = N_META + S
    n_pad = (-T) % ATTN_BLOCK
    Tp = T + n_pad
    tm, tflat = _tiles(B, Tp)
    meta = jnp.broadcast_to(meta_tokens.astype(x.dtype)[None], (B, N_META, D))
    h = jnp.concatenate([jnp.zeros((B, n_pad, D), x.dtype), meta, x], axis=1)
    rope_tabs = _rope_tables(Tp, n_pad)
    weights = (mix_norm_g, w_in, conv_w, conv_b, lru_wa, lru_ba, lru_wx, lru_bx, lru_lambda, q_lora_norm_g,
               w_uq, kv_lora_norm_g, w_ukv, q_head_norm_g, k_head_norm_g, lru_out_norm_g, attn_out_norm_g,
               w_out, ffn_norm_g, peer_wq, peer_subkeys, peer_u, peer_v)
    for l in range(depth):
        p = _layer_params(l, *weights)
        xl, gg, q, k, v = _inproj(h, p, rope_tabs, tm=tm, n_pad=n_pad)
        yl = _lru(xl, gg, p, tt=tm, n_pad=n_pad)
        o = _attention(q, k, v, tq=tm, tk=tm)
        h2, hn, et, gt = _outproj(h, yl, o, p, tm=tm)
        h = _peer(h2.reshape(B * Tp, D), hn.reshape(B * Tp, D), et, gt, p["u"], p["vt"],
                  tm=tflat, te=1024).reshape(B, Tp, D)
    return h[:, n_pad:][:, N_META:]
```

```python
import functools
import math

import jax
import jax.numpy as jnp
from jax import lax
from jax.experimental import pallas as pl
from jax.experimental.pallas import tpu as pltpu

N_META = 16
LRU_WIDTH = 512
LRU_HEADS = 8
LRU_HEAD_DIM = LRU_WIDTH // LRU_HEADS
CONV_WIDTH = 4
LRU_C = 8.0
MLA_HEADS = 8
Q_LORA = 384
KV_LORA = 256
QK_NOPE = 64
QK_ROPE = 32
QK_DIM = QK_NOPE + QK_ROPE
V_DIM = 64
ROPE_THETA = 10000.0
ATTN_BLOCK = 128
PEER_HEADS = 8
N_KEYS = 128
PEER_TOPK = 16
D_SUBKEY = 128
EPS = 1e-6
NEG_INF = -1e30

LANE = 128
SLOT = LANE
MASK_LANE = QK_DIM
ONES_LANE = V_DIM
M_INIT = -3.0e38
VMEM_LIMIT = 56 * 1024 * 1024

F32 = jnp.float32
BF16 = jnp.bfloat16
INV_SQRT2 = 1.0 / math.sqrt(2.0)


def _gelu(x):
    return 0.5 * x * (1.0 + lax.erf(x * INV_SQRT2))


def _rms(x, g, n=None):
    n = x.shape[-1] if n is None else n
    ms = jnp.sum(x * x, axis=-1, keepdims=True) * (1.0 / n)
    return x * lax.rsqrt(ms + EPS) * g


def _dot(a, b):
    return jnp.dot(a, b, preferred_element_type=F32)


def _dot_nt(a, b):
    return lax.dot_general(a, b, (((1,), (1,)), ((), ())), preferred_element_type=F32)


def _params(*sem):
    return pltpu.CompilerParams(dimension_semantics=sem, vmem_limit_bytes=VMEM_LIMIT)


def _inproj_kernel(h_ref, g_ref, win_ref, qg_ref, wuq_ref, kvg_ref, wuk_ref, wuv_ref, qhg_ref, khg_ref,
                   rc_ref, rs1_ref, rs2_ref, xl_ref, gg_ref, q_ref, k_ref, v_ref, *, tm, n_pad):
    h = h_ref[...]
    hn = _rms(h, g_ref[...])
    z = _dot(hn.astype(BF16), win_ref[...])
    o1 = LRU_WIDTH
    o2 = 2 * LRU_WIDTH
    o3 = o2 + Q_LORA
    o4 = o3 + KV_LORA
    xl_ref[...] = z[:, :o1]
    gg_ref[...] = _gelu(z[:, o1:o2])
    cqn = _rms(z[:, o2:o3], qg_ref[...]).astype(BF16)
    ckvn = _rms(z[:, o3:o4], kvg_ref[...]).astype(BF16)
    kr = z[:, o4:o4 + SLOT]
    q = _dot(cqn, wuq_ref[...])
    kn = _dot(ckvn, wuk_ref[...])
    v = _dot(ckvn, wuv_ref[...])
    rc = rc_ref[...]
    rs1 = rs1_ref[...]
    rs2 = rs2_ref[...]
    half = QK_ROPE // 2
    lane = lax.broadcasted_iota(jnp.int32, (1, SLOT), 1)
    mask_lane = (lane == MASK_LANE).astype(F32)
    ones_lane = (lane == ONES_LANE).astype(F32)
    row = pl.program_id(1) * tm + lax.broadcasted_iota(jnp.int32, (tm, 1), 0)
    kbias = jnp.where(row < n_pad, NEG_INF, 0.0) * mask_lane
    scale = 1.0 / math.sqrt(QK_DIM)

    def rope(x):
        return x * rc + pltpu.roll(x, SLOT - half, 1) * rs1 + pltpu.roll(x, half, 1) * rs2

    for hd in range(MLA_HEADS):
        sl = slice(hd * SLOT, (hd + 1) * SLOT)
        qs = rope(_rms(q[:, sl], qhg_ref[...], QK_DIM))
        q_ref[hd] = (qs * scale + mask_lane).astype(BF16)
        ks = rope(_rms(kn[:, sl] + kr, khg_ref[...], QK_DIM))
        k_ref[hd] = (ks + kbias).T.astype(BF16)
        v_ref[hd] = (v[:, sl] + ones_lane).astype(BF16)


def _inproj(h, p, rope_tabs, *, tm, n_pad):
    B, Tp, D = h.shape
    nb = Tp // tm
    H = MLA_HEADS
    full = lambda a: pl.BlockSpec(a.shape, lambda b, i: (0,) * a.ndim)
    tok = lambda w: pl.BlockSpec((None, tm, w), lambda b, i: (b, i, 0))
    head = pl.BlockSpec((None, H, tm, SLOT), lambda b, i: (b, 0, i, 0))
    tab = pl.BlockSpec((tm, SLOT), lambda b, i: (i, 0))
    ws = [p["mix_g"], p["w_in"], p["q_g"], p["w_uq"], p["kv_g"], p["w_uk"], p["w_uv"], p["qh_g"], p["kh_g"]]
    return pl.pallas_call(
        functools.partial(_inproj_kernel, tm=tm, n_pad=n_pad),
        grid=(B, nb),
        in_specs=[tok(D)] + [full(w) for w in ws] + [tab, tab, tab],
        out_specs=[tok(LRU_WIDTH), tok(LRU_WIDTH), head,
                   pl.BlockSpec((None, H, SLOT, tm), lambda b, i: (b, 0, 0, i)), head],
        out_shape=[jax.ShapeDtypeStruct((B, Tp, LRU_WIDTH), F32), jax.ShapeDtypeStruct((B, Tp, LRU_WIDTH), F32),
                   jax.ShapeDtypeStruct((B, H, Tp, SLOT), BF16), jax.ShapeDtypeStruct((B, H, SLOT, Tp), BF16),
                   jax.ShapeDtypeStruct((B, H, Tp, SLOT), BF16)],
        compiler_params=_params("parallel", "parallel"),
        name="inproj",
    )(h, *ws, *rope_tabs)


def _lru_kernel(xl_ref, gg_ref, cw_ref, cb_ref, wa_ref, ba_ref, wx_ref, bx_ref, lam_ref, og_ref, y_ref,
                ext_ref, hst_ref, *, tt, n_pad):
    t = pl.program_id(1)
    HALO = 8

    @pl.when(t == 0)
    def _():
        ext_ref[0:HALO, :] = jnp.zeros((HALO, LRU_WIDTH), F32)
        hst_ref[...] = jnp.zeros((HALO, LRU_WIDTH), F32)

    row = t * tt + lax.broadcasted_iota(jnp.int32, (tt, 1), 0)
    real = row >= n_pad
    ext_ref[HALO:HALO + tt, :] = jnp.where(real, xl_ref[...], 0.0)
    xc = cb_ref[...]
    for k in range(CONV_WIDTH):
        off = HALO - (CONV_WIDTH - 1) + k
        xc = xc + ext_ref[off:off + tt, :] * cw_ref[k:k + 1, :]
    xcb = xc.astype(BF16)
    r = jax.nn.sigmoid(_dot(xcb, wa_ref[...]) + ba_ref[...])
    i = jax.nn.sigmoid(_dot(xcb, wx_ref[...]) + bx_ref[...])
    nl = -lam_ref[...]
    softplus = jnp.maximum(nl, 0.0) + jnp.log1p(jnp.exp(-jnp.abs(nl)))
    log_a = -LRU_C * r * softplus
    a = jnp.exp(log_a)
    mult = jnp.sqrt(1.0 - jnp.exp(2.0 * log_a))
    b = jnp.where(real, mult * (i * xc), 0.0)
    ridx = lax.broadcasted_iota(jnp.int32, (tt, 1), 0)
    s = 1
    while s < tt:
        keep = ridx >= s
        a_sh = jnp.where(keep, pltpu.roll(a, s, 0), 1.0)
        b_sh = jnp.where(keep, pltpu.roll(b, s, 0), 0.0)
        b = a * b_sh + b
        a = a * a_sh
        s *= 2
    hprev = hst_ref[HALO - 1:HALO, :]
    hs = a * hprev + b
    hst_ref[...] = hs[tt - HALO:tt, :]
    ext_ref[0:HALO, :] = ext_ref[tt:tt + HALO, :]
    y = hs * gg_ref[...]
    y_ref[...] = _rms(y, og_ref[...]).astype(BF16)


def _lru(xl, gg, p, *, tt, n_pad):
    B, Tp, W = xl.shape
    full = lambda a: pl.BlockSpec(a.shape, lambda b, i: (0,) * a.ndim)
    tok = pl.BlockSpec((None, tt, W), lambda b, i: (b, i, 0))
    ws = [p["conv_w"], p["conv_b"], p["wa"], p["ba"], p["wx"], p["bx"], p["lam"], p["lru_og"]]
    return pl.pallas_call(
        functools.partial(_lru_kernel, tt=tt, n_pad=n_pad),
        grid=(B, Tp // tt),
        in_specs=[tok, tok] + [full(w) for w in ws],
        out_specs=tok,
        out_shape=jax.ShapeDtypeStruct((B, Tp, W), BF16),
        scratch_shapes=[pltpu.VMEM((tt + 8, W), F32), pltpu.VMEM((8, W), F32)],
        compiler_params=_params("parallel", "arbitrary"),
        name="rglru",
    )(xl, gg, *ws)


HEADS_PER_STEP = SLOT // V_DIM


def _attn_kernel(q_ref, k_ref, v_ref, o_ref, *, tq, tk):
    qi = pl.program_id(2)
    kd = (qi * tq) // tk

    def step(kb, carry, diagonal):
        start = pl.multiple_of(kb * tk, tk)
        out = []
        for hh in range(HEADS_PER_STEP):
            m, acc = carry[hh]
            s = _dot(q_ref[hh], k_ref[hh, :, pl.ds(start, tk)])
            if diagonal:
                qpos = qi * tq + lax.broadcasted_iota(jnp.int32, (tq, tk), 0)
                kpos = kb * tk + lax.broadcasted_iota(jnp.int32, (tq, tk), 1)
                s = jnp.where(kpos <= qpos, s, NEG_INF)
            m_new = jnp.maximum(m, jnp.max(s, axis=-1, keepdims=True))
            alpha = jnp.exp(m - m_new)
            pr = jnp.exp(s - m_new).astype(BF16)
            out.append((m_new, acc * alpha + _dot(pr, v_ref[hh, pl.ds(start, tk), :])))
        return tuple(out)

    init = tuple((jnp.full((tq, 1), M_INIT, F32), jnp.zeros((tq, SLOT), F32)) for _ in range(HEADS_PER_STEP))
    carry = lax.fori_loop(0, kd // 2, lambda kp, c: step(2 * kp + 1, step(2 * kp, c, False), False), init)
    carry = lax.fori_loop(2 * (kd // 2), kd, lambda kb, c: step(kb, c, False), carry)
    carry = step(kd, carry, True)
    lane = lax.broadcasted_iota(jnp.int32, (1, SLOT), 1)
    o = jnp.zeros((tq, SLOT), F32)
    for hh in range(HEADS_PER_STEP):
        acc = carry[hh][1]
        oh = acc / acc[:, ONES_LANE:ONES_LANE + 1]
        if hh:
            oh = pltpu.roll(oh, hh * V_DIM, 1)
        o = jnp.where((lane >= hh * V_DIM) & (lane < (hh + 1) * V_DIM), oh, o)
    o_ref[...] = o.astype(BF16)


def _attention(q, k, v, *, tq, tk):
    B, H, Tp, _ = q.shape
    G = HEADS_PER_STEP
    qspec = pl.BlockSpec((None, G, tq, SLOT), lambda b, h, i: (b, h, i, 0))
    kvspec = pl.BlockSpec((None, G, Tp, SLOT), lambda b, h, i: (b, h, 0, 0))
    return pl.pallas_call(
        functools.partial(_attn_kernel, tq=tq, tk=tk),
        grid=(B, H // G, Tp // tq),
        in_specs=[qspec, pl.BlockSpec((None, G, SLOT, Tp), lambda b, h, i: (b, h, 0, 0)), kvspec],
        out_specs=pl.BlockSpec((None, tq, SLOT), lambda b, h, i: (b, i, h)),
        out_shape=jax.ShapeDtypeStruct((B, Tp, H * V_DIM), BF16),
        compiler_params=_params("parallel", "parallel", "arbitrary"),
        name="attention",
    )(q, k, v)


def _extract_top(x, n, payload=None):
    R, L = x.shape
    pos = lax.broadcasted_iota(jnp.int32, (R, L), 0)
    out_row = lax.broadcasted_iota(jnp.int32, (n, L), 0)
    vals = jnp.zeros((n, L), F32)
    idxs = jnp.zeros((n, L), jnp.int32)
    for it in range(n):
        m = jnp.max(x, axis=0, keepdims=True)
        idx = jnp.min(jnp.where(x == m, pos, R), axis=0, keepdims=True)
        hit = pos == idx
        if payload is None:
            sel = idx
        else:
            sel = jnp.sum(jnp.where(hit, payload, 0), axis=0, keepdims=True)
        vals = jnp.where(out_row == it, m, vals)
        idxs = jnp.where(out_row == it, sel, idxs)
        x = jnp.where(hit, -jnp.inf, x)
    return vals, idxs


def _merge_sort_network(n):
    pairs = []

    def merge(lo, cnt, r):
        step = r * 2
        if step < cnt:
            merge(lo, cnt, step)
            merge(lo + r, cnt, step)
            pairs.extend((i, i + r) for i in range(lo + r, lo + cnt - r, step))
        else:
            pairs.append((lo, lo + r))

    def sort(lo, cnt):
        if cnt > 1:
            sort(lo, cnt // 2)
            sort(lo + cnt // 2, cnt // 2)
            merge(lo, cnt, 1)

    sort(0, n)
    return pairs


def _top_rows(x, n):
    R, L = x.shape
    nv = R // 8
    assert n <= nv
    sub = lax.broadcasted_iota(jnp.int32, (8, L), 0)
    val = [x[8 * v:8 * (v + 1), :] for v in range(nv)]
    row = [sub + 8 * v for v in range(nv)]
    for p, q in _merge_sort_network(nv):
        a, b, ia, ib = val[p], val[q], row[p], row[q]
        swap = (b > a) | ((b == a) & (ib < ia))
        val[p], val[q] = jnp.where(swap, b, a), jnp.where(swap, a, b)
        row[p], row[q] = jnp.where(swap, ib, ia), jnp.where(swap, ia, ib)
    out_row = lax.broadcasted_iota(jnp.int32, (n, L), 0)
    vals = jnp.zeros((n, L), F32)
    idxs = jnp.zeros((n, L), jnp.int32)
    for it in range(n):
        m = jnp.max(val[0], axis=0, keepdims=True)
        idx = jnp.min(jnp.where(val[0] == m, row[0], R), axis=0, keepdims=True)
        vals = jnp.where(out_row == it, m, vals)
        idxs = jnp.where(out_row == it, idx, idxs)
        pop = row[0] == idx
        for v in range(n - it - 1):
            val[v] = jnp.where(pop, val[v + 1], val[v])
            row[v] = jnp.where(pop, row[v + 1], row[v])
    return vals, idxs


def _candidates(ts0, ti0, ts1, ti1):
    K = PEER_TOPK
    sub = lax.broadcasted_iota(jnp.int32, (8, ts0.shape[1]), 0)
    cs = [ts0[0:1, :] + ts1]
    ci = [ti0[0:1, :] * N_KEYS + ti1]
    for a in range(1, 8):
        nb = K // (a + 1)
        piece = ts0[a:a + 1, :] + ts1[0:8, :]
        cs.append(piece if nb >= 8 else jnp.where(sub < nb, piece, -jnp.inf))
        ci.append(ti0[a:a + 1, :] * N_KEYS + ti1[0:8, :])
    cs.append(ts0[8:K, :] + ts1[0:1, :])
    ci.append(ti0[8:K, :] * N_KEYS + ti1[0:1, :])
    return jnp.concatenate(cs, axis=0), jnp.concatenate(ci, axis=0)


def _outproj_kernel(h_ref, yl_ref, o_ref, ag_ref, wol_ref, woa_ref, fg_ref, wq_ref, sk_ref,
                    h2_ref, hn_ref, et_ref, gt_ref, qp_ref, s_ref, *, tm):
    an = _rms(o_ref[...].astype(F32), ag_ref[...]).astype(BF16)
    h2 = h_ref[...] + _dot(yl_ref[...], wol_ref[...]) + _dot(an, woa_ref[...])
    h2_ref[...] = h2
    hn = _rms(h2, fg_ref[...]).astype(BF16)
    hn_ref[...] = hn
    qp_ref[...] = _dot(hn, wq_ref[...])
    K = PEER_TOPK
    nchunk = tm // LANE

    def head_body(hd, carry):
        for pp in range(2):
            off = pl.multiple_of((hd * 2 + pp) * D_SUBKEY, D_SUBKEY)
            qs = qp_ref[:, pl.ds(off, D_SUBKEY)].astype(BF16)
            s_ref[pp] = _dot_nt(sk_ref[pp], qs)

        r0 = pl.multiple_of(hd * K, K)
        for j in range(nchunk):
            lanes = slice(j * LANE, (j + 1) * LANE)
            ts0, ti0 = _top_rows(s_ref[0, :, lanes], K)
            ts1, ti1 = _top_rows(s_ref[1, :, lanes], K)
            cs, ci = _candidates(ts0, ti0, ts1, ti1)
            bs, be = _extract_top(cs, K, payload=ci)
            e = jnp.exp(bs - bs[0:1, :])
            et_ref[pl.ds(r0, K), lanes] = be
            gt_ref[pl.ds(r0, K), lanes] = e / jnp.sum(e, axis=0, keepdims=True)
        return carry

    lax.fori_loop(0, PEER_HEADS, head_body, 0)


def _outproj(h, yl, o, p, *, tm):
    B, Tp, D = h.shape
    nb = Tp // tm
    H = MLA_HEADS
    HK = PEER_HEADS * PEER_TOPK
    full = lambda a: pl.BlockSpec(a.shape, lambda b, i: (0,) * a.ndim)
    tok = lambda w: pl.BlockSpec((None, tm, w), lambda b, i: (b, i, 0))
    sel = pl.BlockSpec((HK, tm), lambda b, i: (0, b * nb + i))
    ws = [p["att_g"], p["w_out_lru"], p["w_out_att"], p["ffn_g"], p["peer_wq"], p["subkeys"]]
    return pl.pallas_call(
        functools.partial(_outproj_kernel, tm=tm),
        grid=(B, nb),
        in_specs=[tok(D), tok(LRU_WIDTH), tok(H * V_DIM)] + [full(w) for w in ws],
        out_specs=[tok(D), tok(D), sel, sel],
        out_shape=[jax.ShapeDtypeStruct((B, Tp, D), F32), jax.ShapeDtypeStruct((B, Tp, D), BF16),
                   jax.ShapeDtypeStruct((HK, B * Tp), jnp.int32), jax.ShapeDtypeStruct((HK, B * Tp), F32)],
        scratch_shapes=[pltpu.VMEM((tm, PEER_HEADS * 2 * D_SUBKEY), F32), pltpu.VMEM((2, N_KEYS, tm), F32)],
        compiler_params=_params("parallel", "parallel"),
        name="outproj_topk",
    )(h, yl, o, *ws)


def _build_slot_tables(c, et_ref, gt_ref, tabi_ref, tabg_ref, eperm_ref, gov_ref, rcnt_ref, *, rcap, rs):
    HK = PEER_HEADS * PEER_TOPK
    NV = HK // 8
    lo = pl.multiple_of(c * LANE, LANE)
    ex = et_ref[:, pl.ds(lo, LANE)]
    gx = gt_ref[:, pl.ds(lo, LANE)]
    ik = ex >> 7
    dk = (ik + (ex & (N_KEYS - 1))) & (N_KEYS - 1)
    dv = [dk[v * 8:(v + 1) * 8, :] for v in range(NV)]
    rank = [jnp.zeros((8, LANE), jnp.int32) for _ in range(NV)]
    sub = lax.broadcasted_iota(jnp.int32, (8, LANE), 0)
    for kp in range(HK - 1):
        v0, r0 = divmod(kp, 8)
        row = dv[v0][r0:r0 + 1, :]
        if r0 < 7:
            rank[v0] = rank[v0] + jnp.where(sub > r0, jnp.where(dv[v0] == row, 1, 0), 0)
        for v in range(v0 + 1, NV):
            rank[v] = rank[v] + jnp.where(dv[v] == row, 1, 0)
    rank = jnp.concatenate(rank, axis=0)
    key = rank * N_KEYS + dk
    rcnt_ref[c] = jnp.max(rank) + 1
    eperm_ref[:, pl.ds(lo, LANE)] = ik * N_KEYS + dk
    gov_ref[:, pl.ds(lo, LANE)] = jnp.where(rank >= rcap, gx, 0.0)
    dio = lax.broadcasted_iota(jnp.int32, (N_KEYS, LANE), 0)

    def round_body(r, carry):
        want = dio + r * N_KEYS
        ti = jnp.full((N_KEYS, LANE), -1, jnp.int32)
        tg = jnp.zeros((N_KEYS, LANE), F32)
        for k in range(HK):
            hit = key[k:k + 1, :] == want
            ti = jnp.where(hit, ik[k:k + 1, :], ti)
            tg = jnp.where(hit, gx[k:k + 1, :], tg)
        r0 = pl.multiple_of(r * N_KEYS, N_KEYS)
        tabi_ref[pl.ds(r0, N_KEYS), pl.ds(lo, LANE)] = ti
        tabg_ref[pl.ds(r0, N_KEYS), pl.ds(lo, LANE)] = tg
        return carry

    nbuilt = jnp.minimum(rcnt_ref[c], rcap)
    lax.fori_loop(0, nbuilt, round_body, 0)

    def empty_body(r, carry):
        r0 = pl.multiple_of(r * N_KEYS, N_KEYS)
        tabi_ref[pl.ds(r0, N_KEYS), pl.ds(lo, LANE)] = jnp.full((N_KEYS, LANE), -1, jnp.int32)
        tabg_ref[pl.ds(r0, N_KEYS), pl.ds(lo, LANE)] = jnp.zeros((N_KEYS, LANE), F32)
        return carry

    lax.fori_loop(nbuilt, rs, empty_body, 0)


def _peer_kernel(hn_ref, et_ref, gt_ref, u_ref, vt_ref, h2_ref, out_ref, pre_ref, act_ref, gx_ref, acc_ref,
                 tabi_ref, tabg_ref, eperm_ref, gov_ref, rcnt_ref, *, tm, te, rcap, rs):
    e = pl.program_id(1)
    HK = PEER_HEADS * PEER_TOPK
    nchunk = tm // LANE
    ntile = te // N_KEYS
    i_base = (e - 1) * ntile

    @pl.when(e == 0)
    def _():
        acc_ref[...] = jnp.zeros(acc_ref.shape, F32)
        pre_ref[...] = jnp.zeros(pre_ref.shape, F32)
        gx_ref[...] = jnp.zeros(gx_ref.shape, F32)

        def build(c, carry):
            _build_slot_tables(c, et_ref, gt_ref, tabi_ref, tabg_ref, eperm_ref, gov_ref, rcnt_ref,
                               rcap=rcap, rs=rs)
            return carry

        lax.fori_loop(0, nchunk, build, 0)

    def deep_chunk(c, carry):
        nround = rcnt_ref[c]

        @pl.when(nround > rs)
        def _():
            lo = pl.multiple_of(c * LANE, LANE)
            ntab = jnp.minimum(nround, rcap)
            nover = jnp.where(nround > rcap, HK // 8, 0)

            def tile_body(it, c2):
                i_abs = i_base + it
                eio = i_abs * N_KEYS + lax.broadcasted_iota(jnp.int32, (N_KEYS, LANE), 0)

                def round_body(r, g):
                    t0 = pl.multiple_of(r * N_KEYS, N_KEYS)
                    ti = tabi_ref[pl.ds(t0, N_KEYS), pl.ds(lo, LANE)]
                    tg = tabg_ref[pl.ds(t0, N_KEYS), pl.ds(lo, LANE)]
                    return g + jnp.where(ti == i_abs, tg, 0.0)

                def over_body(k8, g):
                    k0 = pl.multiple_of(k8 * 8, 8)
                    eb = eperm_ref[pl.ds(k0, 8), pl.ds(lo, LANE)]
                    gb = gov_ref[pl.ds(k0, 8), pl.ds(lo, LANE)]
                    for r in range(8):
                        g = g + jnp.where(eb[r:r + 1, :] == eio, gb[r:r + 1, :], 0.0)
                    return g

                g = lax.fori_loop(rs, ntab, round_body, jnp.zeros((N_KEYS, LANE), F32))
                g = lax.fori_loop(0, nover, over_body, g)
                gx_ref[pl.ds(pl.multiple_of(it * N_KEYS, N_KEYS), N_KEYS), pl.ds(lo, LANE)] = g
                return c2

            lax.fori_loop(0, ntile, tile_body, 0)

        return carry

    lax.fori_loop(0, nchunk, deep_chunk, 0)

    SG = 32
    for c in range(nchunk):
        lanes = slice(c * LANE, (c + 1) * LANE)
        for sg in range(N_KEYS // SG):
            g = [gx_ref[it * N_KEYS + sg * SG:it * N_KEYS + (sg + 1) * SG, lanes] for it in range(ntile)]
            for r in range(rs):
                trow = slice(r * N_KEYS + sg * SG, r * N_KEYS + (sg + 1) * SG)
                ti = tabi_ref[trow, lanes]
                tg = tabg_ref[trow, lanes]
                for it in range(ntile):
                    g[it] = g[it] + jnp.where(ti == i_base + it, tg, 0.0)
            for it in range(ntile):
                rows = slice(it * N_KEYS + sg * SG, it * N_KEYS + (sg + 1) * SG)
                act_ref[rows, lanes] = (_gelu(pre_ref[rows, lanes]) * g[it]).astype(BF16)
    acc_ref[...] += _dot(vt_ref[...], act_ref[...])
    pre_ref[...] = _dot_nt(u_ref[...], hn_ref[...])

    @pl.when(e == pl.num_programs(1) - 1)
    def _():
        out_ref[...] = h2_ref[...] + acc_ref[...].T


def _peer(h2, hn, et, gt, u, vt, *, tm, te, rcap=16, rs=6):
    N, D = h2.shape
    E = u.shape[0]
    HK = et.shape[0]
    ne = E // te
    assert rs <= rcap
    tok = pl.BlockSpec((tm, D), lambda i, e: (i, 0))
    sel = pl.BlockSpec((HK, tm), lambda i, e: (0, i))
    return pl.pallas_call(
        functools.partial(_peer_kernel, tm=tm, te=te, rcap=rcap, rs=rs),
        grid=(N // tm, ne + 1),
        in_specs=[tok, sel, sel, pl.BlockSpec((te, D), lambda i, e: (jnp.minimum(e, ne - 1), 0)),
                  pl.BlockSpec((D, te), lambda i, e: (0, jnp.maximum(e - 1, 0))), tok],
        out_specs=tok,
        out_shape=jax.ShapeDtypeStruct((N, D), F32),
        scratch_shapes=[pltpu.VMEM((te, tm), F32), pltpu.VMEM((te, tm), BF16), pltpu.VMEM((te, tm), F32),
                        pltpu.VMEM((D, tm), F32),
                        pltpu.VMEM((rcap * N_KEYS, tm), jnp.int32), pltpu.VMEM((rcap * N_KEYS, tm), F32),
                        pltpu.VMEM((HK, tm), jnp.int32), pltpu.VMEM((HK, tm), F32),
                        pltpu.SMEM((tm // LANE,), jnp.int32)],
        compiler_params=_params("parallel", "arbitrary"),
        name="peer_experts",
    )(hn, et, gt, u, vt, h2)


def _diag_experts(w):
    n, D = N_KEYS, w.shape[-1]
    w2 = jnp.concatenate([w.reshape(n, n, D)] * 2, axis=1).reshape(2 * n * n, D)
    return w2[n:n + n * (2 * n - 1)].reshape(n, 2 * n - 1, D)[:, :n].reshape(n * n, D)


def _slots(w, width):
    lead = w.shape[:-1]
    w = w.reshape(*lead, -1, width)
    w = jnp.pad(w, [(0, 0)] * len(lead) + [(0, 0), (0, SLOT - width)])
    return w.reshape(*lead, -1)


def _block_diag(w):
    n, hd, _ = w.shape
    eye = jnp.eye(n, dtype=w.dtype)
    return (eye[:, None, :, None] * w[:, :, None, :]).reshape(n * hd, n * hd)


def _rope_tables(Tp, n_pad):
    half = QK_ROPE // 2
    pos = (jnp.arange(Tp) - n_pad).astype(F32)
    freq = ROPE_THETA ** (-jnp.arange(half, dtype=F32) / half)
    ang = pos[:, None] * freq[None, :]
    cos, sin = jnp.cos(ang), jnp.sin(ang)
    z = lambda n: jnp.zeros((Tp, n), F32)
    tail = SLOT - QK_DIM
    rc = jnp.concatenate([jnp.ones((Tp, QK_NOPE), F32), cos, cos, jnp.ones((Tp, tail), F32)], axis=1)
    rs1 = jnp.concatenate([z(QK_NOPE), -sin, z(half), z(tail)], axis=1)
    rs2 = jnp.concatenate([z(QK_NOPE), z(half), sin, z(tail)], axis=1)
    return rc, rs1, rs2


def _layer_params(l, mix_norm_g, w_in, conv_w, conv_b, lru_wa, lru_ba, lru_wx, lru_bx, lru_lambda,
                  q_lora_norm_g, w_uq, kv_lora_norm_g, w_ukv, q_head_norm_g, k_head_norm_g, lru_out_norm_g,
                  attn_out_norm_g, w_out, ffn_norm_g, peer_wq, peer_subkeys, peer_u, peer_v):
    H = MLA_HEADS
    row = lambda a: a[None, :].astype(F32)
    wi = w_in[l]
    split4 = 2 * LRU_WIDTH + Q_LORA + KV_LORA
    kr_cols = jnp.pad(wi[:, split4:], ((0, 0), (QK_NOPE, SLOT - QK_DIM)))
    wkv = w_ukv[l].reshape(KV_LORA, H, QK_NOPE + V_DIM)
    return {
        "mix_g": row(mix_norm_g[l]),
        "w_in": jnp.concatenate([wi[:, :split4], kr_cols], axis=1).astype(BF16),
        "q_g": row(q_lora_norm_g[l]),
        "w_uq": _slots(w_uq[l], QK_DIM).astype(BF16),
        "kv_g": row(kv_lora_norm_g[l]),
        "w_uk": _slots(wkv[:, :, :QK_NOPE].reshape(KV_LORA, H * QK_NOPE), QK_NOPE).astype(BF16),
        "w_uv": _slots(wkv[:, :, QK_NOPE:].reshape(KV_LORA, H * V_DIM), V_DIM).astype(BF16),
        "qh_g": _slots(row(q_head_norm_g[l]), QK_DIM),
        "kh_g": _slots(row(k_head_norm_g[l]), QK_DIM),
        "conv_w": conv_w[l],
        "conv_b": row(conv_b[l]),
        "wa": _block_diag(lru_wa[l]).astype(BF16),
        "ba": row(lru_ba[l].reshape(-1)),
        "wx": _block_diag(lru_wx[l]).astype(BF16),
        "bx": row(lru_bx[l].reshape(-1)),
        "lam": row(lru_lambda[l]),
        "lru_og": row(lru_out_norm_g[l]),
        "att_g": row(attn_out_norm_g[l]),
        "w_out_lru": w_out[l][:LRU_WIDTH].astype(BF16),
        "w_out_att": w_out[l][LRU_WIDTH:].astype(BF16),
        "ffn_g": row(ffn_norm_g[l]),
        "peer_wq": peer_wq[l].astype(BF16),
        "subkeys": peer_subkeys[l].astype(BF16),
        "u": _diag_experts(peer_u[l].astype(BF16)),
        "vt": _diag_experts(peer_v[l].astype(BF16)).T,
    }


def _tiles(B, Tp):
    nblk = Tp // ATTN_BLOCK
    per_tok = ATTN_BLOCK * max(d for d in range(1, 6) if nblk % d == 0)
    n = B * nblk
    flat = ATTN_BLOCK * next(d for d in (4, 2, 5, 3, 1) if n % d == 0)
    return per_tok, flat


def kernel(x, meta_tokens, mix_norm_g, w_in, conv_w, conv_b, lru_wa, lru_ba, lru_wx, lru_bx, lru_lambda,
           q_lora_norm_g, w_uq, kv_lora_norm_g, w_ukv, q_head_norm_g, k_head_norm_g, lru_out_norm_g,
           attn_out_norm_g, w_out, ffn_norm_g, peer_wq, peer_subkeys, peer_u, peer_v):
    B, S, D = x.shape
    depth = w_in.shape[0]
    T = N_META + S
    n_pad = (-T) % ATTN_BLOCK
    Tp = T + n_pad
    tm, tflat = _tiles(B, Tp)
    meta = jnp.broadcast_to(meta_tokens.astype(x.dtype)[None], (B, N_META, D))
    h = jnp.concatenate([jnp.zeros((B, n_pad, D), x.dtype), meta, x], axis=1)
    rope_tabs = _rope_tables(Tp, n_pad)
    weights = (mix_norm_g, w_in, conv_w, conv_b, lru_wa, lru_ba, lru_wx, lru_bx, lru_lambda, q_lora_norm_g,
               w_uq, kv_lora_norm_g, w_ukv, q_head_norm_g, k_head_norm_g, lru_out_norm_g, attn_out_norm_g,
               w_out, ffn_norm_g, peer_wq, peer_subkeys, peer_u, peer_v)
    for l in range(depth):
        p = _layer_params(l, *weights)
        xl, gg, q, k, v = _inproj(h, p, rope_tabs, tm=tm, n_pad=n_pad)
        yl = _lru(xl, gg, p, tt=tm, n_pad=n_pad)
        o = _attention(q, k, v, tq=tm, tk=tm)
        h2, hn, et, gt = _outproj(h, yl, o, p, tm=tm)
        h = _peer(h2.reshape(B * Tp, D), hn.reshape(B * Tp, D), et, gt, p["u"], p["vt"],
                  tm=tflat, te=1024).reshape(B, Tp, D)
    return h[:, n_pad:][:, N_META:]
```
